```python
import math
import jax, jax.numpy as jnp
from jax import lax
import numpy as np

D_MODEL = 1024
BATCH = 16
SEQ = 2048
DEPTH = 1

MEM_LEN = 256
EPS = 1e-6
M_HEADS = 4
M_HEAD_DIM = 256
M_WIDTH = M_HEADS * M_HEAD_DIM
M_CONV = 4
M_CHUNK = 128
F_BIAS_LO = 3.0
F_BIAS_HI = 6.0
A_Q_HEADS = 16
A_KV_HEADS = 2
A_GROUP = A_Q_HEADS // A_KV_HEADS
A_HEAD_DIM = 64
A_WIDTH = A_Q_HEADS * A_HEAD_DIM
A_KV_WIDTH = A_KV_HEADS * A_HEAD_DIM
WINDOW = 128
ROPE_THETA = 10000.0
MIX_WIDTH = M_WIDTH + A_WIDTH
IN_SIZES = (M_WIDTH, M_WIDTH, M_WIDTH, M_WIDTH, M_HEADS, M_HEADS, A_WIDTH, A_KV_WIDTH, A_KV_WIDTH)
IN_WIDTH = 4 * M_WIDTH + 2 * M_HEADS + A_WIDTH + 2 * A_KV_WIDTH
X_HEADS = 4
X_HEAD_DIM = D_MODEL // X_HEADS
D_FF = -(-8 * D_MODEL // (3 * 256)) * 256

kernel_name = "hybrid_mlstm_swa_sink_xattn_layer"


def _split(t, sizes):
    idx = []
    acc = 0
    for s in sizes[:-1]:
        acc += s
        idx.append(acc)
    return jnp.split(t, idx, axis=-1)


def rmsnorm(x, g):
    xf = x.astype(jnp.float32)
    r = xf * lax.rsqrt(jnp.mean(xf * xf, axis=-1, keepdims=True) + EPS)
    return (r * g.astype(jnp.float32)).astype(x.dtype)


def causal_depthwise_conv(x, w):
    K, C = w.shape
    return lax.conv_general_dilated(
        x, w[:, None, :].astype(x.dtype), window_strides=(1,), padding=[(K - 1, 0)],
        dimension_numbers=("NWC", "WIO", "NWC"), feature_group_count=C)


def rope_tables(positions):
    inv = ROPE_THETA ** (-jnp.arange(0, A_HEAD_DIM, 2, dtype=jnp.float32) / A_HEAD_DIM)
    ang = positions.astype(jnp.float32)[..., None] * inv
    return jnp.cos(ang)[:, :, None, :], jnp.sin(ang)[:, :, None, :]


def apply_rope(t, cos, sin):
    tf = t.astype(jnp.float32)
    t1, t2 = jnp.split(tf, 2, axis=-1)
    return jnp.concatenate([t1 * cos - t2 * sin, t2 * cos + t1 * sin], axis=-1).astype(t.dtype)


def mlstm_chunkwise(q, k, v, logf, ig):
    B, S, H, D = q.shape
    nc = S // M_CHUNK

    def chunks(t):
        t = t.reshape((B, nc, M_CHUNK, H) + t.shape[3:])
        return jnp.moveaxis(t, (1, 3), (0, 2))

    causal = jnp.tril(jnp.ones((M_CHUNK, M_CHUNK), dtype=bool))

    def step(carry, xs):
        C, n, m = carry
        qc, kc, vc, lf, ic = xs
        b = jnp.cumsum(lf, axis=-1)
        g = b + m[..., None]
        Dm = b[..., :, None] - b[..., None, :] + ic[..., None, :]
        Dm = jnp.where(causal, Dm, -jnp.inf)
        mj = jnp.maximum(g, jnp.max(Dm, axis=-1))
        s = jnp.einsum("bhjd,bhid->bhji", qc, kc) * jnp.exp(Dm - mj[..., None])
        wg = jnp.exp(g - mj)
        num = wg[..., None] * jnp.einsum("bhjd,bhde->bhje", qc, C) + jnp.einsum("bhji,bhie->bhje", s, vc)
        den = wg * jnp.einsum("bhjd,bhd->bhj", qc, n) + jnp.sum(s, axis=-1)
        h = num / jnp.maximum(jnp.abs(den), jnp.exp(-mj))[..., None]
        bL = b[..., -1]
        w = bL[..., None] - b + ic
        m_new = jnp.maximum(bL + m, jnp.max(w, axis=-1))
        decay = jnp.exp(bL + m - m_new)
        wi = jnp.exp(w - m_new[..., None])
        C_new = decay[..., None, None] * C + jnp.einsum("bhi,bhid,bhie->bhde", wi, kc, vc)
        n_new = decay[..., None] * n + jnp.einsum("bhi,bhid->bhd", wi, kc)
        return (C_new, n_new, m_new), h

    init = (jnp.zeros((B, H, D, D), jnp.float32), jnp.zeros((B, H, D), jnp.float32),
            jnp.zeros((B, H), jnp.float32))
    _, h = lax.scan(step, init, (chunks(q), chunks(k), chunks(v), chunks(logf), chunks(ig)))
    return jnp.transpose(h, (1, 0, 3, 2, 4)).reshape(B, S, H, D)


def sliding_window_attention(q, k, v, sinks):
    B, S = q.shape[0], q.shape[1]
    nb = S // WINDOW
    qb = q.reshape(B, nb, WINDOW, A_KV_HEADS, A_GROUP, A_HEAD_DIM)
    kb = k.reshape(B, nb, WINDOW, A_KV_HEADS, A_HEAD_DIM)
    vb = v.reshape(B, nb, WINDOW, A_KV_HEADS, A_HEAD_DIM)

    def with_prev(t):
        prev = jnp.pad(t, ((0, 0), (1, 0), (0, 0), (0, 0), (0, 0)))[:, :-1]
        return jnp.concatenate([prev, t], axis=2)

    kw, vw = with_prev(kb), with_prev(vb)
    qpos = jnp.arange(WINDOW)[:, None]
    kpos = jnp.arange(2 * WINDOW)[None, :]
    diff = WINDOW + qpos - kpos
    band = (diff >= 0) & (diff < WINDOW)
    sink = sinks.astype(jnp.float32).reshape(1, A_KV_HEADS, A_GROUP, 1, 1)
    scale = A_HEAD_DIM ** -0.5

    def block(args):
        qi, ki, vi, bi = args
        s = jnp.einsum("bqhgd,bkhd->bhgqk", qi.astype(jnp.float32), ki.astype(jnp.float32)) * scale
        valid = band & ((bi > 0) | (kpos >= WINDOW))
        s = jnp.where(valid, s, -jnp.inf)
        mx = jnp.maximum(jnp.max(s, axis=-1, keepdims=True), sink)
        p = jnp.exp(s - mx)
        denom = jnp.sum(p, axis=-1, keepdims=True) + jnp.exp(sink - mx)
        o = jnp.einsum("bhgqk,bkhd->bqhgd", p / denom, vi.astype(jnp.float32))
        return o.astype(qi.dtype)

    out = lax.map(block, (jnp.moveaxis(qb, 1, 0), jnp.moveaxis(kw, 1, 0), jnp.moveaxis(vw, 1, 0),
                          jnp.arange(nb)))
    return jnp.moveaxis(out, 0, 1).reshape(B, S, A_WIDTH)


def token_mixer(u, positions, w_in, conv_qk, f_bias, i_bias, mlstm_norm_g, attn_sinks, w_out):
    B, S, _ = u.shape
    proj = u @ w_in
    qm, km, vm, om, ipre, fpre, qa, ka, va = _split(proj, IN_SIZES)

    qk = jax.nn.silu(causal_depthwise_conv(jnp.concatenate([qm, km], axis=-1), conv_qk))
    qm, km = jnp.split(qk, 2, axis=-1)
    qh = qm.astype(jnp.float32).reshape(B, S, M_HEADS, M_HEAD_DIM)
    kh = km.astype(jnp.float32).reshape(B, S, M_HEADS, M_HEAD_DIM) * (M_HEAD_DIM ** -0.5)
    vh = vm.astype(jnp.float32).reshape(B, S, M_HEADS, M_HEAD_DIM)
    logf = jax.nn.log_sigmoid(fpre.astype(jnp.float32) + f_bias.astype(jnp.float32))
    ig = ipre.astype(jnp.float32) + i_bias.astype(jnp.float32)
    hm = mlstm_chunkwise(qh, kh, vh, logf, ig)
    hm = hm * lax.rsqrt(jnp.mean(hm * hm, axis=-1, keepdims=True) + EPS)
    hm = hm * mlstm_norm_g.astype(jnp.float32).reshape(M_HEADS, M_HEAD_DIM)
    hm = (jax.nn.sigmoid(om.astype(jnp.float32)) * hm.reshape(B, S, M_WIDTH)).astype(u.dtype)

    cos, sin = rope_tables(positions)
    qa = apply_rope(qa.reshape(B, S, A_Q_HEADS, A_HEAD_DIM), cos, sin)
    ka = apply_rope(ka.reshape(B, S, A_KV_HEADS, A_HEAD_DIM), cos, sin)
    va = va.reshape(B, S, A_KV_HEADS, A_HEAD_DIM)
    ha = sliding_window_attention(qa, ka, va, attn_sinks)

    return jnp.concatenate([hm, ha], axis=-1) @ w_out


def memory_cross_attention(u, mem_n, w_xq, w_xkv, w_xo):
    B, S, _ = u.shape
    q = (u @ w_xq).reshape(B, S, X_HEADS, X_HEAD_DIM).astype(jnp.float32)
    k, v = jnp.split(mem_n @ w_xkv, 2, axis=-1)
    k = k.reshape(B, MEM_LEN, X_HEADS, X_HEAD_DIM).astype(jnp.float32)
    v = v.reshape(B, MEM_LEN, X_HEADS, X_HEAD_DIM).astype(jnp.float32)
    p = jax.nn.softmax(jnp.einsum("bshd,bmhd->bhsm", q, k) * (X_HEAD_DIM ** -0.5), axis=-1)
    o = jnp.einsum("bhsm,bmhd->bshd", p, v).reshape(B, S, D_MODEL).astype(u.dtype)
    return o @ w_xo


def swiglu_ffn(u, w_gate_up, w_down):
    g, up = jnp.split(u @ w_gate_up, 2, axis=-1)
    return (jax.nn.silu(g) * up) @ w_down


def setup_inputs(seed: int = 0) -> dict:
    key = jax.random.key(seed)
    ks = jax.random.split(key, 24)
    f32 = jnp.float32

    def nrm(k, shape, scale):
        return jax.random.normal(k, shape, f32) * scale

    def gain(k):
        return 1.0 + 0.02 * jax.random.normal(k, (DEPTH, D_MODEL), f32)

    x = jax.random.normal(ks[0], (BATCH, SEQ, D_MODEL), f32)
    mem = jax.random.normal(ks[1], (BATCH, MEM_LEN, D_MODEL), f32)
    start = jax.random.randint(ks[2], (BATCH, 1), 0, 4096, dtype=jnp.int32)
    positions = start + jnp.arange(SEQ, dtype=jnp.int32)[None, :]
    f_bias = jnp.linspace(F_BIAS_LO, F_BIAS_HI, M_HEADS, dtype=f32)[None, :] + nrm(ks[8], (DEPTH, M_HEADS), 0.1)
    return {
        "x": x,
        "mem": mem,
        "positions": positions,
        "mix_pre_g": gain(ks[3]),
        "mix_post_g": gain(ks[4]),
        "w_in": nrm(ks[5], (DEPTH, D_MODEL, IN_WIDTH), D_MODEL ** -0.5),
        "conv_qk": nrm(ks[6], (DEPTH, M_CONV, 2 * M_WIDTH), M_CONV ** -0.5),
        "f_bias": f_bias,
        "i_bias": nrm(ks[9], (DEPTH, M_HEADS), 0.1),
        "mlstm_norm_g": 1.0 + 0.02 * jax.random.normal(ks[10], (DEPTH, M_WIDTH), f32),
        "attn_sinks": nrm(ks[11], (DEPTH, A_Q_HEADS), 0.5),
        "w_out": nrm(ks[12], (DEPTH, MIX_WIDTH, D_MODEL), MIX_WIDTH ** -0.5),
        "xattn_pre_g": gain(ks[13]),
        "xattn_post_g": gain(ks[14]),
        "mem_norm_g": gain(ks[15]),
        "w_xq": nrm(ks[16], (DEPTH, D_MODEL, D_MODEL), D_MODEL ** -0.5),
        "w_xkv": nrm(ks[17], (DEPTH, D_MODEL, 2 * D_MODEL), D_MODEL ** -0.5),
        "w_xo": nrm(ks[18], (DEPTH, D_MODEL, D_MODEL), D_MODEL ** -0.5),
        "ffn_pre_g": gain(ks[19]),
        "ffn_post_g": gain(ks[20]),
        "w_gate_up": nrm(ks[21], (DEPTH, D_MODEL, 2 * D_FF), D_MODEL ** -0.5),
        "w_down": nrm(ks[22], (DEPTH, D_FF, D_MODEL), D_FF ** -0.5),
    }


def reference(x, mem, positions, mix_pre_g, mix_post_g, w_in, conv_qk, f_bias, i_bias, mlstm_norm_g,
              attn_sinks, w_out, xattn_pre_g, xattn_post_g, mem_norm_g, w_xq, w_xkv, w_xo,
              ffn_pre_g, ffn_post_g, w_gate_up, w_down):
    h = x
    for l in range(DEPTH):
        a = token_mixer(rmsnorm(h, mix_pre_g[l]), positions, w_in[l], conv_qk[l], f_bias[l], i_bias[l],
                        mlstm_norm_g[l], attn_sinks[l], w_out[l])
        h = h + rmsnorm(a, mix_post_g[l])
        c = memory_cross_attention(rmsnorm(h, xattn_pre_g[l]), rmsnorm(mem, mem_norm_g[l]),
                                   w_xq[l], w_xkv[l], w_xo[l])
        h = h + rmsnorm(c, xattn_post_g[l])
        f = swiglu_ffn(rmsnorm(h, ffn_pre_g[l]), w_gate_up[l], w_down[l])
        h = h + rmsnorm(f, ffn_post_g[l])
    return h
```

```python
import functools

import jax
import jax.numpy as jnp
from jax import lax
from jax.experimental import pallas as pl
from jax.experimental.pallas import tpu as pltpu

D_MODEL = 1024
EPS = 1e-6
M_HEADS = 4
M_HEAD_DIM = 256
M_WIDTH = M_HEADS * M_HEAD_DIM
M_CONV = 4
M_CHUNK = 128
A_Q_HEADS = 16
A_KV_HEADS = 2
A_HEAD_DIM = 64
A_WIDTH = A_Q_HEADS * A_HEAD_DIM
A_KV_WIDTH = A_KV_HEADS * A_HEAD_DIM
WINDOW = 128
ROPE_THETA = 10000.0
X_HEADS = 4
X_HEAD_DIM = D_MODEL // X_HEADS
LANES = 128
CONV_HALO = 8
VMEM_LIMIT = 56 * 1024 * 1024

F32 = jnp.float32
BF16 = jnp.bfloat16


def _rms(x, g):
    return x * lax.rsqrt(jnp.mean(x * x, axis=-1, keepdims=True) + EPS) * g


def _dot(a, b):
    return jnp.dot(a, b, preferred_element_type=F32)


def _dot_nt(a, b):
    return lax.dot_general(a, b, (((1,), (1,)), ((), ())), preferred_element_type=F32)


def _dot_tn(a, b):
    return lax.dot_general(a, b, (((0,), (0,)), ((), ())), preferred_element_type=F32)


def _const_spec(shape):
    nd = len(shape)
    return pl.BlockSpec(shape, lambda *_: (0,) * nd)


def _in_proj_kernel(x_ref, g_ref, w_ref, qk_ref, vo_ref, qa_ref, kva_ref, gate_ref):
    u = _rms(x_ref[...], g_ref[...]).astype(BF16)
    off = 0
    for ref in (qk_ref, vo_ref, qa_ref, kva_ref, gate_ref):
        width = ref.shape[-1]
        ref[...] = _dot(u, w_ref[:, off:off + width]).astype(ref.dtype)
        off += width


def _in_proj(x2, g, w_cat, tm):
    t = x2.shape[0]
    widths = (2 * M_WIDTH, 2 * M_WIDTH, A_WIDTH, 2 * A_KV_WIDTH, LANES)
    dtypes = (BF16, BF16, BF16, BF16, F32)
    return pl.pallas_call(
        _in_proj_kernel,
        grid=(t // tm,),
        in_specs=[pl.BlockSpec((tm, D_MODEL), lambda i: (i, 0)),
                  _const_spec((1, D_MODEL)),
                  _const_spec(w_cat.shape)],
        out_specs=[pl.BlockSpec((tm, w), lambda i: (i, 0)) for w in widths],
        out_shape=[jax.ShapeDtypeStruct((t, w), d) for w, d in zip(widths, dtypes)],
        compiler_params=pltpu.CompilerParams(dimension_semantics=("arbitrary",),
                                             vmem_limit_bytes=VMEM_LIMIT),
        name="in_proj",
    )(x2, g, w_cat)


def _mlstm_kernel(qk_ref, vo_ref, gate_ref, conv_ref, bias_ref, ng_ref, out_ref,
                  hist_ref, c_ref, n_ref, m_ref):
    L = M_CHUNK
    D = M_HEAD_DIM

    @pl.when(pl.program_id(1) == 0)
    def _():
        hist_ref[0:CONV_HALO, :] = jnp.zeros((CONV_HALO, 2 * M_WIDTH), F32)
        c_ref[...] = jnp.zeros_like(c_ref)
        n_ref[...] = jnp.zeros_like(n_ref)
        m_ref[...] = jnp.zeros_like(m_ref)

    hist_ref[CONV_HALO:CONV_HALO + L, :] = qk_ref[...].astype(F32)

    gates = gate_ref[...] + bias_ref[...]
    logf = jax.nn.log_sigmoid(gates)
    row = lax.broadcasted_iota(jnp.int32, (L, L), 0)
    col = lax.broadcasted_iota(jnp.int32, (L, L), 1)
    causal = col <= row
    tril = jnp.where(causal, 1.0, 0.0).astype(F32)
    bcum = jnp.dot(tril, logf, precision=lax.Precision.HIGHEST,
                   preferred_element_type=F32)
    gates_t = gates.T
    bcum_t = bcum.T

    def conv_silu(c0):
        acc = None
        for j in range(M_CONV):
            r0 = CONV_HALO - (M_CONV - 1) + j
            term = hist_ref[r0:r0 + L, c0:c0 + D] * conv_ref[j:j + 1, c0:c0 + D]
            acc = term if acc is None else acc + term
        return acc * jax.nn.sigmoid(acc)

    for h in range(M_HEADS):
        c0 = h * D
        q = conv_silu(c0)
        k = conv_silu(M_WIDTH + c0) * (M_HEAD_DIM ** -0.5)
        qb = q.astype(BF16)
        kb = k.astype(BF16)
        v = vo_ref[:, c0:c0 + D]
        og = vo_ref[:, M_WIDTH + c0:M_WIDTH + c0 + D].astype(F32)

        b_col = bcum[:, M_HEADS + h:M_HEADS + h + 1]
        b_row = bcum_t[M_HEADS + h:M_HEADS + h + 1, :]
        i_col = gates[:, h:h + 1]
        i_row = gates_t[h:h + 1, :]
        m_prev = m_ref[h:h + 1, 0:1]

        g_col = b_col + m_prev
        dm = jnp.where(causal, b_col - b_row + i_row, -jnp.inf)
        mj = jnp.maximum(g_col, jnp.max(dm, axis=1, keepdims=True))
        s = _dot_nt(qb, kb) * jnp.exp(dm - mj)
        wg = jnp.exp(g_col - mj)
        inter = _dot(qb, c_ref[h].astype(BF16))
        intra = _dot(s.astype(BF16), v)
        num = wg * inter + intra
        qn = jnp.sum(q * n_ref[h:h + 1, :], axis=1, keepdims=True)
        den = wg * qn + jnp.sum(s, axis=1, keepdims=True)
        hh = num / jnp.maximum(jnp.abs(den), jnp.exp(-mj))
        hh = hh * lax.rsqrt(jnp.mean(hh * hh, axis=1, keepdims=True) + EPS)
        hh = hh * ng_ref[:, c0:c0 + D]
        out_ref[:, c0:c0 + D] = (jax.nn.sigmoid(og) * hh).astype(out_ref.dtype)

        b_last = b_col[L - 1:L, :]
        w_col = b_last - b_col + i_col
        m_new = jnp.maximum(b_last + m_prev, jnp.max(w_col, axis=0, keepdims=True))
        decay = jnp.exp(b_last + m_prev - m_new)
        kw = k * jnp.exp(w_col - m_new)
        c_ref[h] = decay * c_ref[h] + _dot_tn(kw.astype(BF16), v)
        n_ref[h:h + 1, :] = decay * n_ref[h:h + 1, :] + jnp.sum(kw, axis=0, keepdims=True)
        m_ref[h:h + 1, :] = jnp.broadcast_to(m_new, (1, LANES))

    hist_ref[0:CONV_HALO, :] = hist_ref[L:L + CONV_HALO, :]


def _mlstm(qk, vo, gates, conv_qk, gate_bias, norm_g, batch, seq):
    nc = seq // M_CHUNK
    blk = lambda b, c: (b * nc + c, 0)
    return pl.pallas_call(
        _mlstm_kernel,
        grid=(batch, nc),
        in_specs=[pl.BlockSpec((M_CHUNK, 2 * M_WIDTH), blk),
                  pl.BlockSpec((M_CHUNK, 2 * M_WIDTH), blk),
                  pl.BlockSpec((M_CHUNK, LANES), blk),
                  _const_spec((M_CONV, 2 * M_WIDTH)),
                  _const_spec((1, LANES)),
                  _const_spec((1, M_WIDTH))],
        out_specs=pl.BlockSpec((M_CHUNK, M_WIDTH), blk),
        out_shape=jax.ShapeDtypeStruct((batch * seq, M_WIDTH), BF16),
        scratch_shapes=[pltpu.VMEM((M_CHUNK + CONV_HALO, 2 * M_WIDTH), F32),
                        pltpu.VMEM((M_HEADS, M_HEAD_DIM, M_HEAD_DIM), F32),
                        pltpu.VMEM((8, M_HEAD_DIM), F32),
                        pltpu.VMEM((8, LANES), F32)],
        compiler_params=pltpu.CompilerParams(dimension_semantics=("arbitrary", "arbitrary"),
                                             vmem_limit_bytes=VMEM_LIMIT),
        name="mlstm",
    )(qk, vo, gates, conv_qk, gate_bias, norm_g)


def _swa_kernel(qa_ref, kva_ref, pos_ref, inv_ref, sink_ref, out_ref, k_ref, v_ref):
    W = WINDOW
    half = A_HEAD_DIM // 2

    @pl.when(pl.program_id(1) == 0)
    def _():
        k_ref[:, 0:W, :] = jnp.zeros((4, W, LANES), BF16)
        v_ref[:, 0:W, :] = jnp.zeros((4, W, LANES), BF16)

    @pl.when(pl.program_id(1) > 0)
    def _():
        k_ref[:, 0:W, :] = k_ref[:, W:2 * W, :]
        v_ref[:, 0:W, :] = v_ref[:, W:2 * W, :]

    lane = lax.broadcasted_iota(jnp.int32, (W, LANES), 1)
    first = (lane % A_HEAD_DIM) < half
    low = lane < A_HEAD_DIM
    ang = pos_ref[...].astype(F32) * inv_ref[...]
    cos_t = jnp.cos(ang)
    sin_t = jnp.sin(ang)
    sin_s = jnp.where(first, -sin_t, sin_t)

    def rope(t, c, s):
        partner = jnp.where(first, pltpu.roll(t, LANES - half, 1), pltpu.roll(t, half, 1))
        return t * c + partner * s

    kf = rope(kva_ref[:, 0:LANES].astype(F32), cos_t, sin_s)
    vf = kva_ref[:, LANES:2 * LANES].astype(F32)
    for src, dst in ((kf, k_ref), (vf, v_ref)):
        swapped = pltpu.roll(src, A_HEAD_DIM, 1)
        dst[0, W:2 * W, :] = jnp.where(low, src, 0.0).astype(BF16)
        dst[1, W:2 * W, :] = jnp.where(low, 0.0, swapped).astype(BF16)
        dst[2, W:2 * W, :] = jnp.where(low, swapped, 0.0).astype(BF16)
        dst[3, W:2 * W, :] = jnp.where(low, 0.0, src).astype(BF16)

    scale = A_HEAD_DIM ** -0.5
    cos_q = cos_t * scale
    sin_q = sin_s * scale
    qpos = lax.broadcasted_iota(jnp.int32, (W, 2 * W), 0)
    kpos = lax.broadcasted_iota(jnp.int32, (W, 2 * W), 1)
    diff = W + qpos - kpos
    valid = (diff >= 0) & (diff < W) & ((pl.program_id(1) > 0) | (kpos >= W))
    bias = jnp.where(valid, 0.0, -jnp.inf).astype(F32)

    pairs = A_Q_HEADS // A_KV_HEADS // 2
    for g in range(A_KV_HEADS):
        qs = []
        for p in range(pairs):
            c0 = (g * pairs + p) * LANES
            qs.append(rope(qa_ref[:, c0:c0 + LANES].astype(F32), cos_q, sin_q).astype(BF16))
        qst = jnp.concatenate(qs, axis=0)
        acc = None
        inv_den = []
        for s_ in range(2):
            sc = _dot_nt(qst, k_ref[2 * g + s_])
            ps = []
            for p in range(pairs):
                hidx = 2 * (g * pairs + p) + s_
                sink = sink_ref[0:1, hidx:hidx + 1]
                sh = sc[p * W:(p + 1) * W, :] + bias
                mx = jnp.maximum(jnp.max(sh, axis=1, keepdims=True), sink)
                pe = jnp.exp(sh - mx)
                den = jnp.sum(pe, axis=1, keepdims=True) + jnp.exp(sink - mx)
                inv_den.append(1.0 / den)
                ps.append(pe.astype(BF16))
            o = _dot(jnp.concatenate(ps, axis=0), v_ref[2 * g + s_])
            acc = o if acc is None else acc + o
        for p in range(pairs):
            c0 = (g * pairs + p) * LANES
            sc_ = jnp.where(low, inv_den[p], inv_den[pairs + p])
            out_ref[:, c0:c0 + LANES] = (acc[p * W:(p + 1) * W, :] * sc_).astype(out_ref.dtype)


def _swa(qa, kva, pos, inv_freq, sinks, batch, seq):
    nb = seq // WINDOW
    blk = lambda b, c: (b * nb + c, 0)
    return pl.pallas_call(
        _swa_kernel,
        grid=(batch, nb),
        in_specs=[pl.BlockSpec((WINDOW, A_WIDTH), blk),
                  pl.BlockSpec((WINDOW, 2 * A_KV_WIDTH), blk),
                  pl.BlockSpec((WINDOW, 1), blk),
                  _const_spec((1, LANES)),
                  _const_spec((1, LANES))],
        out_specs=pl.BlockSpec((WINDOW, A_WIDTH), blk),
        out_shape=jax.ShapeDtypeStruct((batch * seq, A_WIDTH), BF16),
        scratch_shapes=[pltpu.VMEM((4, 2 * WINDOW, LANES), BF16),
                        pltpu.VMEM((4, 2 * WINDOW, LANES), BF16)],
        compiler_params=pltpu.CompilerParams(dimension_semantics=("arbitrary", "arbitrary"),
                                             vmem_limit_bytes=VMEM_LIMIT),
        name="swa",
    )(qa, kva, pos, inv_freq, sinks)


def _mix_out_kernel(hm_ref, ha_ref, x_ref, wm_ref, wa_ref, gpost_ref, gpre_ref, wq_ref,
                    h1_ref, xq_ref):
    a = _dot(hm_ref[...], wm_ref[...]) + _dot(ha_ref[...], wa_ref[...])
    h1 = x_ref[...] + _rms(a, gpost_ref[...])
    h1_ref[...] = h1
    u = _rms(h1, gpre_ref[...]).astype(BF16)
    xq_ref[...] = (_dot(u, wq_ref[...]) * (X_HEAD_DIM ** -0.5)).astype(xq_ref.dtype)


def _mix_out(hm, ha, x2, w_m, w_a, g_post, g_pre, w_xq, tm):
    t = x2.shape[0]
    row = lambda i: (i, 0)
    return pl.pallas_call(
        _mix_out_kernel,
        grid=(t // tm,),
        in_specs=[pl.BlockSpec((tm, M_WIDTH), row),
                  pl.BlockSpec((tm, A_WIDTH), row),
                  pl.BlockSpec((tm, D_MODEL), row),
                  _const_spec(w_m.shape), _const_spec(w_a.shape),
                  _const_spec((1, D_MODEL)), _const_spec((1, D_MODEL)),
                  _const_spec(w_xq.shape)],
        out_specs=[pl.BlockSpec((tm, D_MODEL), row), pl.BlockSpec((tm, D_MODEL), row)],
        out_shape=[jax.ShapeDtypeStruct((t, D_MODEL), F32),
                   jax.ShapeDtypeStruct((t, D_MODEL), BF16)],
        compiler_params=pltpu.CompilerParams(dimension_semantics=("arbitrary",),
                                             vmem_limit_bytes=VMEM_LIMIT),
        name="mix_out",
    )(hm, ha, x2, w_m, w_a, g_post, g_pre, w_xq)


def _mem_kv_kernel(mem_ref, g_ref, w_ref, kv_ref):
    u = _rms(mem_ref[...], g_ref[...]).astype(BF16)
    kv_ref[...] = _dot(u, w_ref[...]).astype(kv_ref.dtype)


def _mem_kv(mem2, g, w_xkv, mem_len):
    t = mem2.shape[0]
    return pl.pallas_call(
        _mem_kv_kernel,
        grid=(t // mem_len,),
        in_specs=[pl.BlockSpec((mem_len, D_MODEL), lambda i: (i, 0)),
                  _const_spec((1, D_MODEL)),
                  _const_spec(w_xkv.shape)],
        out_specs=pl.BlockSpec((mem_len, 2 * D_MODEL), lambda i: (i, 0)),
        out_shape=jax.ShapeDtypeStruct((t, 2 * D_MODEL), BF16),
        compiler_params=pltpu.CompilerParams(dimension_semantics=("arbitrary",),
                                             vmem_limit_bytes=VMEM_LIMIT),
        name="mem_kv",
    )(mem2, g, w_xkv)


def _xattn_ffn_kernel(h1_ref, xq_ref, kv_ref, wo_ref, gxpost_ref, gfpre_ref, wgu_ref, wd_ref,
                      gfpost_ref, out_ref, o_ref, *, ff_chunk):
    d_ff = wd_ref.shape[0]
    for h in range(X_HEADS):
        c0 = h * X_HEAD_DIM
        s = _dot_nt(xq_ref[:, c0:c0 + X_HEAD_DIM], kv_ref[:, c0:c0 + X_HEAD_DIM])
        e = jnp.exp(s - jnp.max(s, axis=1, keepdims=True))
        p = e / jnp.sum(e, axis=1, keepdims=True)
        o = _dot(p.astype(BF16), kv_ref[:, D_MODEL + c0:D_MODEL + c0 + X_HEAD_DIM])
        o_ref[:, c0:c0 + X_HEAD_DIM] = o.astype(BF16)
    c = _dot(o_ref[...], wo_ref[...])
    h2 = h1_ref[...] + _rms(c, gxpost_ref[...])
    u = _rms(h2, gfpre_ref[...]).astype(BF16)
    f = None
    for j in range(d_ff // ff_chunk):
        j0 = j * ff_chunk
        gate = _dot(u, wgu_ref[:, j0:j0 + ff_chunk])
        up = _dot(u, wgu_ref[:, d_ff + j0:d_ff + j0 + ff_chunk])
        act = (gate * jax.nn.sigmoid(gate) * up).astype(BF16)
        part = _dot(act, wd_ref[j0:j0 + ff_chunk, :])
        f = part if f is None else f + part
    out_ref[...] = h2 + _rms(f, gfpost_ref[...])


def _xattn_ffn(h1, xq, kv, w_xo, g_xpost, g_fpre, w_gu, w_d, g_fpost, tm, seq, mem_len, ff_chunk):
    t = h1.shape[0]
    row = lambda i: (i, 0)
    per_b = seq // tm
    return pl.pallas_call(
        functools.partial(_xattn_ffn_kernel, ff_chunk=ff_chunk),
        grid=(t // tm,),
        in_specs=[pl.BlockSpec((tm, D_MODEL), row),
                  pl.BlockSpec((tm, D_MODEL), row),
                  pl.BlockSpec((mem_len, 2 * D_MODEL), lambda i: (i // per_b, 0)),
                  _const_spec(w_xo.shape),
                  _const_spec((1, D_MODEL)), _const_spec((1, D_MODEL)),
                  _const_spec(w_gu.shape), _const_spec(w_d.shape),
                  _const_spec((1, D_MODEL))],
        out_specs=pl.BlockSpec((tm, D_MODEL), row),
        out_shape=jax.ShapeDtypeStruct((t, D_MODEL), F32),
        scratch_shapes=[pltpu.VMEM((tm, D_MODEL), BF16)],
        compiler_params=pltpu.CompilerParams(dimension_semantics=("arbitrary",),
                                             vmem_limit_bytes=VMEM_LIMIT),
        name="xattn_ffn",
    )(h1, xq, kv, w_xo, g_xpost, g_fpre, w_gu, w_d, g_fpost)


def _layer(h, mem2, pos2, inv_freq, p, batch, seq, mem_len):
    w_in = p["w_in"]
    o_i = 4 * M_WIDTH
    o_qa = o_i + 2 * M_HEADS
    o_ka = o_qa + A_WIDTH
    w_gate = jnp.pad(w_in[:, o_i:o_qa], ((0, 0), (0, LANES - 2 * M_HEADS)))
    w_cat = jnp.concatenate([w_in[:, :o_i], w_in[:, o_qa:], w_gate], axis=1).astype(BF16)
    gate_bias = jnp.pad(jnp.concatenate([p["i_bias"], p["f_bias"]]),
                        (0, LANES - 2 * M_HEADS)).reshape(1, LANES)
    sinks = jnp.pad(p["attn_sinks"], (0, LANES - A_Q_HEADS)).reshape(1, LANES)
    row = lambda v: v.reshape(1, -1)

    qk, vo, qa, kva, gates = _in_proj(h, row(p["mix_pre_g"]), w_cat, tm=512)
    hm = _mlstm(qk, vo, gates, p["conv_qk"], gate_bias, row(p["mlstm_norm_g"]), batch, seq)
    ha = _swa(qa, kva, pos2, inv_freq, sinks, batch, seq)
    w_out = p["w_out"].astype(BF16)
    h1, xq = _mix_out(hm, ha, h, w_out[:M_WIDTH], w_out[M_WIDTH:], row(p["mix_post_g"]),
                      row(p["xattn_pre_g"]), p["w_xq"].astype(BF16), tm=512)
    kv = _mem_kv(mem2, row(p["mem_norm_g"]), p["w_xkv"].astype(BF16), mem_len)
    return _xattn_ffn(h1, xq, kv, p["w_xo"].astype(BF16), row(p["xattn_post_g"]),
                      row(p["ffn_pre_g"]), p["w_gate_up"].astype(BF16), p["w_down"].astype(BF16),
                      row(p["ffn_post_g"]), tm=256, seq=seq, mem_len=mem_len, ff_chunk=256)


def kernel(x, mem, positions, mix_pre_g, mix_post_g, w_in, conv_qk, f_bias, i_bias, mlstm_norm_g,
           attn_sinks, w_out, xattn_pre_g, xattn_post_g, mem_norm_g, w_xq, w_xkv, w_xo,
           ffn_pre_g, ffn_post_g, w_gate_up, w_down):
    batch, seq, _ = x.shape
    mem_len = mem.shape[1]
    depth = w_in.shape[0]
    params = dict(mix_pre_g=mix_pre_g, mix_post_g=mix_post_g, w_in=w_in, conv_qk=conv_qk,
                  f_bias=f_bias, i_bias=i_bias, mlstm_norm_g=mlstm_norm_g, attn_sinks=attn_sinks,
                  w_out=w_out, xattn_pre_g=xattn_pre_g, xattn_post_g=xattn_post_g,
                  mem_norm_g=mem_norm_g, w_xq=w_xq, w_xkv=w_xkv, w_xo=w_xo, ffn_pre_g=ffn_pre_g,
                  ffn_post_g=ffn_post_g, w_gate_up=w_gate_up, w_down=w_down)
    inv = ROPE_THETA ** (-jnp.arange(0, A_HEAD_DIM, 2, dtype=F32) / A_HEAD_DIM)
    inv_freq = jnp.tile(inv, LANES // inv.shape[0]).reshape(1, LANES)
    h = x.reshape(batch * seq, D_MODEL)
    mem2 = mem.reshape(batch * mem_len, D_MODEL)
    pos2 = positions.reshape(batch * seq, 1)
    for l in range(depth):
        h = _layer(h, mem2, pos2, inv_freq, {k: v[l] for k, v in params.items()},
                   batch, seq, mem_len)
    return h.reshape(batch, seq, D_MODEL)
```

```python
import functools

import jax
import jax.numpy as jnp
from jax import lax
from jax.experimental import pallas as pl
from jax.experimental.pallas import tpu as pltpu

D_MODEL = 1024
EPS = 1e-6
M_HEADS = 4
M_HEAD_DIM = 256
M_WIDTH = M_HEADS * M_HEAD_DIM
M_CONV = 4
M_CHUNK = 128
A_Q_HEADS = 16
A_KV_HEADS = 2
A_HEAD_DIM = 64
A_WIDTH = A_Q_HEADS * A_HEAD_DIM
A_KV_WIDTH = A_KV_HEADS * A_HEAD_DIM
WINDOW = 128
ROPE_THETA = 10000.0
X_HEADS = 4
X_HEAD_DIM = D_MODEL // X_HEADS
LANES = 128
CONV_HALO = 8
VMEM_LIMIT = 56 * 1024 * 1024

F32 = jnp.float32
BF16 = jnp.bfloat16


def _rms(x, g):
    return x * lax.rsqrt(jnp.mean(x * x, axis=-1, keepdims=True) + EPS) * g


def _dot(a, b):
    return jnp.dot(a, b, preferred_element_type=F32)


def _dot_nt(a, b):
    return lax.dot_general(a, b, (((1,), (1,)), ((), ())), preferred_element_type=F32)


def _dot_tn(a, b):
    return lax.dot_general(a, b, (((0,), (0,)), ((), ())), preferred_element_type=F32)


def _const_spec(shape):
    nd = len(shape)
    return pl.BlockSpec(shape, lambda *_: (0,) * nd, pipeline_mode=pl.Buffered(1))


def _in_proj_kernel(x_ref, g_ref, pos_ref, inv_ref, w_ref, qk_ref, vo_ref, qa_ref, kva_ref,
                    gate_ref):
    u = _rms(x_ref[...], g_ref[...]).astype(BF16)
    off = 0
    for ref in (qk_ref, vo_ref):
        width = ref.shape[-1]
        ref[...] = _dot(u, w_ref[:, off:off + width]).astype(ref.dtype)
        off += width

    tm = x_ref.shape[0]
    lane = lax.broadcasted_iota(jnp.int32, (tm, LANES), 1)
    ang = pos_ref[...].astype(F32) * inv_ref[...]
    cos_t = jnp.cos(ang)
    sin_t = jnp.where(lane < LANES // 2, -jnp.sin(ang), jnp.sin(ang))

    def rope(t, c, s):
        return t * c + pltpu.roll(t, LANES // 2, 1) * s

    scale = A_HEAD_DIM ** -0.5
    cos_q = cos_t * scale
    sin_q = sin_t * scale
    for j in range(A_WIDTH // LANES):
        t = _dot(u, w_ref[:, off:off + LANES])
        qa_ref[:, j * LANES:(j + 1) * LANES] = rope(t, cos_q, sin_q).astype(qa_ref.dtype)
        off += LANES
    kv = _dot(u, w_ref[:, off:off + 2 * LANES])
    kva_ref[:, 0:LANES] = rope(kv[:, 0:LANES], cos_t, sin_t).astype(kva_ref.dtype)
    kva_ref[:, LANES:2 * LANES] = kv[:, LANES:2 * LANES].astype(kva_ref.dtype)
    off += 2 * LANES
    gate_ref[...] = _dot(u, w_ref[:, off:off + LANES])


def _in_proj(x2, g, pos, inv_freq, w_cat, tm):
    t = x2.shape[0]
    widths = (2 * M_WIDTH, 2 * M_WIDTH, A_WIDTH, 2 * A_KV_WIDTH, LANES)
    dtypes = (BF16, BF16, BF16, BF16, F32)
    return pl.pallas_call(
        _in_proj_kernel,
        grid=(t // tm,),
        in_specs=[pl.BlockSpec((tm, D_MODEL), lambda i: (i, 0)),
                  _const_spec((1, D_MODEL)),
                  pl.BlockSpec((tm, 1), lambda i: (i, 0)),
                  _const_spec((1, LANES)),
                  _const_spec(w_cat.shape)],
        out_specs=[pl.BlockSpec((tm, w), lambda i: (i, 0)) for w in widths],
        out_shape=[jax.ShapeDtypeStruct((t, w), d) for w, d in zip(widths, dtypes)],
        compiler_params=pltpu.CompilerParams(dimension_semantics=("arbitrary",),
                                             vmem_limit_bytes=VMEM_LIMIT),
        name="in_proj",
    )(x2, g, pos, inv_freq, w_cat)


def _mlstm_kernel(qk_ref, vo_ref, gate_ref, conv_ref, bias_ref, ng_ref, out_ref,
                  hist_ref, c_ref, n_ref, m_ref):
    L = M_CHUNK
    D = M_HEAD_DIM

    @pl.when(pl.program_id(1) == 0)
    def _():
        hist_ref[0:CONV_HALO, :] = jnp.zeros((CONV_HALO, 2 * M_WIDTH), F32)
        c_ref[...] = jnp.zeros_like(c_ref)
        n_ref[...] = jnp.zeros_like(n_ref)
        m_ref[...] = jnp.zeros_like(m_ref)

    hist_ref[CONV_HALO:CONV_HALO + L, :] = qk_ref[...].astype(F32)

    gates = gate_ref[...] + bias_ref[...]
    logf = jax.nn.log_sigmoid(gates)
    row = lax.broadcasted_iota(jnp.int32, (L, L), 0)
    col = lax.broadcasted_iota(jnp.int32, (L, L), 1)
    causal = col <= row
    tril = jnp.where(causal, 1.0, 0.0).astype(F32)
    bcum = jnp.dot(tril, logf, precision=lax.Precision.HIGHEST,
                   preferred_element_type=F32)
    gates_t = gates.T
    bcum_t = bcum.T

    def conv_silu(c0):
        acc = None
        for j in range(M_CONV):
            r0 = CONV_HALO - (M_CONV - 1) + j
            term = hist_ref[r0:r0 + L, c0:c0 + D] * conv_ref[j:j + 1, c0:c0 + D]
            acc = term if acc is None else acc + term
        return acc * jax.nn.sigmoid(acc)

    for h in range(M_HEADS):
        c0 = h * D
        q = conv_silu(c0)
        k = conv_silu(M_WIDTH + c0) * (M_HEAD_DIM ** -0.5)
        qb = q.astype(BF16)
        kb = k.astype(BF16)
        v = vo_ref[:, c0:c0 + D]
        og = vo_ref[:, M_WIDTH + c0:M_WIDTH + c0 + D].astype(F32)

        b_col = bcum[:, M_HEADS + h:M_HEADS + h + 1]
        b_row = bcum_t[M_HEADS + h:M_HEADS + h + 1, :]
        i_col = gates[:, h:h + 1]
        i_row = gates_t[h:h + 1, :]
        m_prev = m_ref[h:h + 1, 0:1]

        g_col = b_col + m_prev
        dm = jnp.where(causal, b_col - b_row + i_row, -jnp.inf)
        mj = jnp.maximum(g_col, jnp.max(dm, axis=1, keepdims=True))
        s = _dot_nt(qb, kb) * jnp.exp(dm - mj)
        wg = jnp.exp(g_col - mj)
        inter = _dot(qb, c_ref[h].astype(BF16))
        intra = _dot(s.astype(BF16), v)
        num = wg * inter + intra
        qn = jnp.sum(q * n_ref[h:h + 1, :], axis=1, keepdims=True)
        den = wg * qn + jnp.sum(s, axis=1, keepdims=True)
        hh = num / jnp.maximum(jnp.abs(den), jnp.exp(-mj))
        hh = hh * lax.rsqrt(jnp.mean(hh * hh, axis=1, keepdims=True) + EPS)
        hh = hh * ng_ref[:, c0:c0 + D]
        out_ref[:, c0:c0 + D] = (jax.nn.sigmoid(og) * hh).astype(out_ref.dtype)

        b_last = b_col[L - 1:L, :]
        w_col = b_last - b_col + i_col
        m_new = jnp.maximum(b_last + m_prev, jnp.max(w_col, axis=0, keepdims=True))
        decay = jnp.exp(b_last + m_prev - m_new)
        kw = k * jnp.exp(w_col - m_new)
        c_ref[h] = decay * c_ref[h] + _dot_tn(kw.astype(BF16), v)
        n_ref[h:h + 1, :] = decay * n_ref[h:h + 1, :] + jnp.sum(kw, axis=0, keepdims=True)
        m_ref[h:h + 1, :] = jnp.broadcast_to(m_new, (1, LANES))

    hist_ref[0:CONV_HALO, :] = hist_ref[L:L + CONV_HALO, :]


def _mlstm(qk, vo, gates, conv_qk, gate_bias, norm_g, batch, seq):
    nc = seq // M_CHUNK
    blk = lambda b, c: (b * nc + c, 0)
    return pl.pallas_call(
        _mlstm_kernel,
        grid=(batch, nc),
        in_specs=[pl.BlockSpec((M_CHUNK, 2 * M_WIDTH), blk),
                  pl.BlockSpec((M_CHUNK, 2 * M_WIDTH), blk),
                  pl.BlockSpec((M_CHUNK, LANES), blk),
                  _const_spec((M_CONV, 2 * M_WIDTH)),
                  _const_spec((1, LANES)),
                  _const_spec((1, M_WIDTH))],
        out_specs=pl.BlockSpec((M_CHUNK, M_WIDTH), blk),
        out_shape=jax.ShapeDtypeStruct((batch * seq, M_WIDTH), BF16),
        scratch_shapes=[pltpu.VMEM((M_CHUNK + CONV_HALO, 2 * M_WIDTH), F32),
                        pltpu.VMEM((M_HEADS, M_HEAD_DIM, M_HEAD_DIM), F32),
                        pltpu.VMEM((8, M_HEAD_DIM), F32),
                        pltpu.VMEM((8, LANES), F32)],
        compiler_params=pltpu.CompilerParams(dimension_semantics=("arbitrary", "arbitrary"),
                                             vmem_limit_bytes=VMEM_LIMIT),
        name="mlstm",
    )(qk, vo, gates, conv_qk, gate_bias, norm_g)


def _swa_kernel(sink_ref, qa_ref, kva_ref, out_ref, k_ref, vt_ref, ot_ref):
    W = WINDOW
    half = A_HEAD_DIM // 2

    @pl.when(pl.program_id(1) == 0)
    def _():
        for s in range(4):
            k_ref[s * 2 * W:s * 2 * W + W, :] = jnp.zeros((W, LANES), BF16)
        vt_ref[:, 0:W] = jnp.zeros((LANES, W), BF16)

    @pl.when(pl.program_id(1) > 0)
    def _():
        for s in range(4):
            k_ref[s * 2 * W:s * 2 * W + W, :] = k_ref[s * 2 * W + W:(s + 1) * 2 * W, :]
        vt_ref[:, 0:W] = vt_ref[:, W:2 * W]

    lane = lax.broadcasted_iota(jnp.int32, (W, LANES), 1)
    slot_a = (lane % A_HEAD_DIM) < half
    kf = kva_ref[:, 0:LANES].astype(F32)
    variants = (jnp.where(slot_a, kf, 0.0),
                jnp.where(slot_a, 0.0, pltpu.roll(kf, half, 1)),
                jnp.where(slot_a, pltpu.roll(kf, LANES - half, 1), 0.0),
                jnp.where(slot_a, 0.0, kf))
    for s, var in enumerate(variants):
        k_ref[s * 2 * W + W:(s + 1) * 2 * W, :] = var.astype(BF16)
    vt_ref[:, W:2 * W] = kva_ref[:, LANES:2 * LANES].astype(F32).T.astype(BF16)

    kpos = lax.broadcasted_iota(jnp.int32, (2 * W, W), 0)
    qpos = lax.broadcasted_iota(jnp.int32, (2 * W, W), 1)
    diff = W + qpos - kpos
    valid = (diff >= 0) & (diff < W) & ((pl.program_id(1) > 0) | (kpos >= W))
    bias = jnp.where(valid, 0.0, -jnp.inf).astype(F32)

    n_pairs = A_Q_HEADS // 2
    pairs_per_group = n_pairs // A_KV_HEADS

    def scores(pair):
        g = pair // pairs_per_group
        q = qa_ref[:, pair * LANES:(pair + 1) * LANES]
        return _dot_nt(k_ref[g * 4 * W:(g + 1) * 4 * W, :], q)

    st_pair = scores(0)
    for pair in range(n_pairs):
        g = pair // pairs_per_group
        st_next = scores(pair + 1) if pair + 1 < n_pairs else None
        pes, inv_dens = [], []
        for slot in range(2):
            sink = sink_ref[2 * pair + slot]
            st = st_pair[slot * 2 * W:(slot + 1) * 2 * W, :] + bias
            mx = jnp.maximum(jnp.max(st, axis=0, keepdims=True), sink)
            pe = jnp.exp(st - mx)
            den = jnp.sum(pe, axis=0, keepdims=True) + jnp.exp(sink - mx)
            pes.append(pe.astype(BF16))
            inv_dens.append(1.0 / den)
        ot = _dot(vt_ref[g * A_HEAD_DIM:(g + 1) * A_HEAD_DIM, :],
                  jnp.concatenate(pes, axis=1))
        for slot in range(2):
            head = 2 * pair + slot
            ot_ref[head * A_HEAD_DIM:(head + 1) * A_HEAD_DIM, :] = (
                ot[:, slot * W:(slot + 1) * W] * inv_dens[slot])
        st_pair = st_next

    for j in range(A_WIDTH // LANES):
        out_ref[:, j * LANES:(j + 1) * LANES] = (
            ot_ref[j * LANES:(j + 1) * LANES, :].T.astype(out_ref.dtype))


def _swa(qa, kva, sinks, batch, seq):
    nb = seq // WINDOW
    blk = lambda b, c: (b * nb + c, 0)
    return pl.pallas_call(
        _swa_kernel,
        grid=(batch, nb),
        in_specs=[pl.BlockSpec(memory_space=pltpu.SMEM),
                  pl.BlockSpec((WINDOW, A_WIDTH), blk),
                  pl.BlockSpec((WINDOW, 2 * A_KV_WIDTH), blk)],
        out_specs=pl.BlockSpec((WINDOW, A_WIDTH), blk),
        out_shape=jax.ShapeDtypeStruct((batch * seq, A_WIDTH), BF16),
        scratch_shapes=[pltpu.VMEM((4 * 2 * WINDOW, LANES), BF16),
                        pltpu.VMEM((LANES, 2 * WINDOW), BF16),
                        pltpu.VMEM((A_WIDTH, WINDOW), F32)],
        compiler_params=pltpu.CompilerParams(dimension_semantics=("arbitrary", "arbitrary"),
                                             vmem_limit_bytes=VMEM_LIMIT),
        name="swa",
    )(sinks, qa, kva)


def _mix_out_kernel(hm_ref, ha_ref, x_ref, wm_ref, wa_ref, gpost_ref, gpre_ref, wq_ref,
                    h1_ref, xq_ref):
    a = _dot(hm_ref[...], wm_ref[...]) + _dot(ha_ref[...], wa_ref[...])
    h1 = x_ref[...] + _rms(a, gpost_ref[...])
    h1_ref[...] = h1
    u = _rms(h1, gpre_ref[...]).astype(BF16)
    xq_ref[...] = (_dot(u, wq_ref[...]) * (X_HEAD_DIM ** -0.5)).astype(xq_ref.dtype)


def _mix_out(hm, ha, x2, w_m, w_a, g_post, g_pre, w_xq, tm):
    t = x2.shape[0]
    row = lambda i: (i, 0)
    return pl.pallas_call(
        _mix_out_kernel,
        grid=(t // tm,),
        in_specs=[pl.BlockSpec((tm, M_WIDTH), row),
                  pl.BlockSpec((tm, A_WIDTH), row),
                  pl.BlockSpec((tm, D_MODEL), row),
                  _const_spec(w_m.shape), _const_spec(w_a.shape),
                  _const_spec((1, D_MODEL)), _const_spec((1, D_MODEL)),
                  _const_spec(w_xq.shape)],
        out_specs=[pl.BlockSpec((tm, D_MODEL), row), pl.BlockSpec((tm, D_MODEL), row)],
        out_shape=[jax.ShapeDtypeStruct((t, D_MODEL), F32),
                   jax.ShapeDtypeStruct((t, D_MODEL), BF16)],
        compiler_params=pltpu.CompilerParams(dimension_semantics=("arbitrary",),
                                             vmem_limit_bytes=VMEM_LIMIT),
        name="mix_out",
    )(hm, ha, x2, w_m, w_a, g_post, g_pre, w_xq)


def _mem_kv_kernel(mem_ref, g_ref, w_ref, kv_ref):
    u = _rms(mem_ref[...], g_ref[...]).astype(BF16)
    kv_ref[...] = _dot(u, w_ref[...]).astype(kv_ref.dtype)


def _mem_kv(mem2, g, w_xkv, mem_len):
    t = mem2.shape[0]
    return pl.pallas_call(
        _mem_kv_kernel,
        grid=(t // mem_len,),
        in_specs=[pl.BlockSpec((mem_len, D_MODEL), lambda i: (i, 0)),
                  _const_spec((1, D_MODEL)),
                  _const_spec(w_xkv.shape)],
        out_specs=pl.BlockSpec((mem_len, 2 * D_MODEL), lambda i: (i, 0)),
        out_shape=jax.ShapeDtypeStruct((t, 2 * D_MODEL), BF16),
        compiler_params=pltpu.CompilerParams(dimension_semantics=("arbitrary",),
                                             vmem_limit_bytes=VMEM_LIMIT),
        name="mem_kv",
    )(mem2, g, w_xkv)


def _xattn_ffn_kernel(h1_ref, xq_ref, kv_ref, wo_ref, gxpost_ref, gfpre_ref, wgu_ref, wd_ref,
                      gfpost_ref, out_ref, o_ref, *, ff_chunk):
    d_ff = wd_ref.shape[0]

    def scores(h):
        c0 = h * X_HEAD_DIM
        return _dot_nt(xq_ref[:, c0:c0 + X_HEAD_DIM], kv_ref[:, c0:c0 + X_HEAD_DIM])

    s = scores(0)
    for h in range(X_HEADS):
        c0 = h * X_HEAD_DIM
        s_next = scores(h + 1) if h + 1 < X_HEADS else None
        e = jnp.exp(s - jnp.max(s, axis=1, keepdims=True))
        inv = 1.0 / jnp.sum(e, axis=1, keepdims=True)
        o = _dot(e.astype(BF16), kv_ref[:, D_MODEL + c0:D_MODEL + c0 + X_HEAD_DIM])
        o_ref[:, c0:c0 + X_HEAD_DIM] = (o * inv).astype(BF16)
        s = s_next
    c = _dot(o_ref[...], wo_ref[...])
    h2 = h1_ref[...] + _rms(c, gxpost_ref[...])
    u = _rms(h2, gfpre_ref[...]).astype(BF16)
    f = None
    for j in range(d_ff // ff_chunk):
        j0 = j * ff_chunk
        gate = _dot(u, wgu_ref[:, j0:j0 + ff_chunk])
        up = _dot(u, wgu_ref[:, d_ff + j0:d_ff + j0 + ff_chunk])
        act = (gate * jax.nn.sigmoid(gate) * up).astype(BF16)
        part = _dot(act, wd_ref[j0:j0 + ff_chunk, :])
        f = part if f is None else f + part
    out_ref[...] = h2 + _rms(f, gfpost_ref[...])


def _xattn_ffn(h1, xq, kv, w_xo, g_xpost, g_fpre, w_gu, w_d, g_fpost, tm, seq, mem_len, ff_chunk):
    t = h1.shape[0]
    row = lambda i: (i, 0)
    per_b = seq // tm
    return pl.pallas_call(
        functools.partial(_xattn_ffn_kernel, ff_chunk=ff_chunk),
        grid=(t // tm,),
        in_specs=[pl.BlockSpec((tm, D_MODEL), row),
                  pl.BlockSpec((tm, D_MODEL), row),
                  pl.BlockSpec((mem_len, 2 * D_MODEL), lambda i: (i // per_b, 0)),
                  _const_spec(w_xo.shape),
                  _const_spec((1, D_MODEL)), _const_spec((1, D_MODEL)),
                  _const_spec(w_gu.shape), _const_spec(w_d.shape),
                  _const_spec((1, D_MODEL))],
        out_specs=pl.BlockSpec((tm, D_MODEL), row),
        out_shape=jax.ShapeDtypeStruct((t, D_MODEL), F32),
        scratch_shapes=[pltpu.VMEM((tm, D_MODEL), BF16)],
        compiler_params=pltpu.CompilerParams(dimension_semantics=("arbitrary",),
                                             vmem_limit_bytes=VMEM_LIMIT),
        name="xattn_ffn",
    )(h1, xq, kv, w_xo, g_xpost, g_fpre, w_gu, w_d, g_fpost)


def _layer(h, mem2, pos2, inv_freq, p, batch, seq, mem_len):
    w_in = p["w_in"]
    o_i = 4 * M_WIDTH
    o_qa = o_i + 2 * M_HEADS
    o_ka = o_qa + A_WIDTH
    w_gate = jnp.pad(w_in[:, o_i:o_qa], ((0, 0), (0, LANES - 2 * M_HEADS)))

    def pair_interleave(w):
        w5 = w.reshape(w.shape[0], -1, 2, 2, A_HEAD_DIM // 2)
        return jnp.swapaxes(w5, 2, 3).reshape(w.shape)

    w_cat = jnp.concatenate([w_in[:, :o_i], pair_interleave(w_in[:, o_qa:o_ka]),
                             pair_interleave(w_in[:, o_ka:o_ka + A_KV_WIDTH]),
                             w_in[:, o_ka + A_KV_WIDTH:], w_gate], axis=1).astype(BF16)
    gate_bias = jnp.pad(jnp.concatenate([p["i_bias"], p["f_bias"]]),
                        (0, LANES - 2 * M_HEADS)).reshape(1, LANES)
    row = lambda v: v.reshape(1, -1)

    qk, vo, qa, kva, gates = _in_proj(h, row(p["mix_pre_g"]), pos2, inv_freq, w_cat, tm=512)
    hm = _mlstm(qk, vo, gates, p["conv_qk"], gate_bias, row(p["mlstm_norm_g"]), batch, seq)
    ha = _swa(qa, kva, p["attn_sinks"], batch, seq)
    w_out = p["w_out"].astype(BF16)
    h1, xq = _mix_out(hm, ha, h, w_out[:M_WIDTH], w_out[M_WIDTH:], row(p["mix_post_g"]),
                      row(p["xattn_pre_g"]), p["w_xq"].astype(BF16), tm=512)
    kv = _mem_kv(mem2, row(p["mem_norm_g"]), p["w_xkv"].astype(BF16), mem_len)
    return _xattn_ffn(h1, xq, kv, p["w_xo"].astype(BF16), row(p["xattn_post_g"]),
                      row(p["ffn_pre_g"]), p["w_gate_up"].astype(BF16), p["w_down"].astype(BF16),
                      row(p["ffn_post_g"]), tm=512, seq=seq, mem_len=mem_len, ff_chunk=1408)


def kernel(x, mem, positions, mix_pre_g, mix_post_g, w_in, conv_qk, f_bias, i_bias, mlstm_norm_g,
           attn_sinks, w_out, xattn_pre_g, xattn_post_g, mem_norm_g, w_xq, w_xkv, w_xo,
           ffn_pre_g, ffn_post_g, w_gate_up, w_down):
    batch, seq, _ = x.shape
    mem_len = mem.shape[1]
    depth = w_in.shape[0]
    params = dict(mix_pre_g=mix_pre_g, mix_post_g=mix_post_g, w_in=w_in, conv_qk=conv_qk,
                  f_bias=f_bias, i_bias=i_bias, mlstm_norm_g=mlstm_norm_g, attn_sinks=attn_sinks,
                  w_out=w_out, xattn_pre_g=xattn_pre_g, xattn_post_g=xattn_post_g,
                  mem_norm_g=mem_norm_g, w_xq=w_xq, w_xkv=w_xkv, w_xo=w_xo, ffn_pre_g=ffn_pre_g,
                  ffn_post_g=ffn_post_g, w_gate_up=w_gate_up, w_down=w_down)
    inv = ROPE_THETA ** (-jnp.arange(0, A_HEAD_DIM, 2, dtype=F32) / A_HEAD_DIM)
    inv_freq = jnp.tile(inv, LANES // inv.shape[0]).reshape(1, LANES)
    h = x.reshape(batch * seq, D_MODEL)
    mem2 = mem.reshape(batch * mem_len, D_MODEL)
    pos2 = positions.reshape(batch * seq, 1)
    for l in range(depth):
        h = _layer(h, mem2, pos2, inv_freq, {k: v[l] for k, v in params.items()},
                   batch, seq, mem_len)
    return h.reshape(batch, seq, D_MODEL)
```

```python
import functools

import jax
import jax.numpy as jnp
from jax import lax
from jax.experimental import pallas as pl
from jax.experimental.pallas import tpu as pltpu

D_MODEL = 1024
EPS = 1e-6
M_HEADS = 4
M_HEAD_DIM = 256
M_WIDTH = M_HEADS * M_HEAD_DIM
M_CONV = 4
M_CHUNK = 128
A_Q_HEADS = 16
A_KV_HEADS = 2
A_HEAD_DIM = 64
A_WIDTH = A_Q_HEADS * A_HEAD_DIM
A_KV_WIDTH = A_KV_HEADS * A_HEAD_DIM
WINDOW = 128
ROPE_THETA = 10000.0
X_HEADS = 4
X_HEAD_DIM = D_MODEL // X_HEADS
LANES = 128
CONV_HALO = 16
SCORE_LOOKAHEAD = 3
VMEM_LIMIT = 56 * 1024 * 1024

F32 = jnp.float32
BF16 = jnp.bfloat16


def _rms(x, g):
    return x * lax.rsqrt(jnp.mean(x * x, axis=-1, keepdims=True) + EPS) * g


def _dot(a, b):
    return jnp.dot(a, b, preferred_element_type=F32)


def _dot_nt(a, b):
    return lax.dot_general(a, b, (((1,), (1,)), ((), ())), preferred_element_type=F32)


def _dot_tn(a, b):
    return lax.dot_general(a, b, (((0,), (0,)), ((), ())), preferred_element_type=F32)


def _const_spec(shape):
    nd = len(shape)
    return pl.BlockSpec(shape, lambda *_: (0,) * nd, pipeline_mode=pl.Buffered(1))


def _in_proj_kernel(x_ref, g_ref, pos_ref, inv_ref, w_ref, qk_ref, vo_ref, qa_ref, kva_ref,
                    gate_ref):
    tm = x_ref.shape[0]
    u = _rms(x_ref[...], g_ref[...]).astype(BF16)
    off = 0
    for ref in (qk_ref, vo_ref):
        width = ref.shape[-1]
        ref[...] = _dot(u, w_ref[:, off:off + width]).astype(ref.dtype)
        off += width

    n_freq = A_HEAD_DIM // 2
    n_seg = LANES // n_freq
    rows = tm // n_seg
    seg = lax.broadcasted_iota(jnp.int32, (rows, LANES), 1) // n_freq
    pos_f = pos_ref[...].astype(F32)
    pos_c = jnp.broadcast_to(pos_f[0:rows], (rows, LANES))
    for s in range(1, n_seg):
        pos_c = jnp.where(seg == s, pos_f[s * rows:(s + 1) * rows], pos_c)
    ang_c = pos_c * inv_ref[...]

    def spread(table_c):
        rolled = [table_c] + [pltpu.roll(table_c, n_freq * k, 1) for k in range(1, n_seg)]
        chunks = []
        for s in range(n_seg):
            full = rolled[(0 - s) % n_seg]
            for p in range(1, n_seg):
                full = jnp.where(seg == p, rolled[(p - s) % n_seg], full)
            chunks.append(full)
        return jnp.concatenate(chunks, axis=0)

    lane = lax.broadcasted_iota(jnp.int32, (tm, LANES), 1)
    cos_t = spread(jnp.cos(ang_c))
    sin_t = spread(jnp.sin(ang_c))
    sin_t = jnp.where(lane < LANES // 2, -sin_t, sin_t)

    def rope(t, c, s):
        return t * c + pltpu.roll(t, LANES // 2, 1) * s

    scale = A_HEAD_DIM ** -0.5
    cos_q = cos_t * scale
    sin_q = sin_t * scale
    for j in range(A_WIDTH // (2 * LANES)):
        t = _dot(u, w_ref[:, off:off + 2 * LANES])
        for i in range(2):
            c0 = (2 * j + i) * LANES
            qa_ref[:, c0:c0 + LANES] = rope(t[:, i * LANES:(i + 1) * LANES],
                                            cos_q, sin_q).astype(qa_ref.dtype)
        off += 2 * LANES
    kv = _dot(u, w_ref[:, off:off + 2 * LANES])
    kva_ref[:, 0:LANES] = rope(kv[:, 0:LANES], cos_t, sin_t).astype(kva_ref.dtype)
    kva_ref[:, LANES:2 * LANES] = kv[:, LANES:2 * LANES].astype(kva_ref.dtype)
    off += 2 * LANES
    gate_ref[...] = _dot(u, w_ref[:, off:off + LANES])


def _in_proj(x2, g, pos, inv_freq, w_cat, tm):
    t = x2.shape[0]
    widths = (2 * M_WIDTH, 2 * M_WIDTH, A_WIDTH, 2 * A_KV_WIDTH, LANES)
    dtypes = (BF16, BF16, BF16, BF16, F32)
    return pl.pallas_call(
        _in_proj_kernel,
        grid=(t // tm,),
        in_specs=[pl.BlockSpec((tm, D_MODEL), lambda i: (i, 0)),
                  _const_spec((1, D_MODEL)),
                  pl.BlockSpec((tm, 1), lambda i: (i, 0)),
                  _const_spec((1, LANES)),
                  _const_spec(w_cat.shape)],
        out_specs=[pl.BlockSpec((tm, w), lambda i: (i, 0)) for w in widths],
        out_shape=[jax.ShapeDtypeStruct((t, w), d) for w, d in zip(widths, dtypes)],
        compiler_params=pltpu.CompilerParams(dimension_semantics=("arbitrary",),
                                             vmem_limit_bytes=VMEM_LIMIT),
        name="in_proj",
    )(x2, g, pos, inv_freq, w_cat)


def _mlstm_kernel(qk_ref, vo_ref, gate_ref, conv_ref, bias_ref, ngb_ref, out_ref,
                  hist_ref, ct_ref, n_ref, m_ref):
    L = M_CHUNK
    D = M_HEAD_DIM

    @pl.when(pl.program_id(1) == 0)
    def _():
        hist_ref[0:CONV_HALO, :] = jnp.zeros((CONV_HALO, 2 * M_WIDTH), BF16)
        ct_ref[...] = jnp.zeros_like(ct_ref)
        n_ref[...] = jnp.zeros_like(n_ref)
        m_ref[...] = jnp.zeros_like(m_ref)

    hist_ref[CONV_HALO:CONV_HALO + L, :] = qk_ref[...]

    sel_r = lax.broadcasted_iota(jnp.int32, (L, CONV_HALO + L), 0)
    sel_c = lax.broadcasted_iota(jnp.int32, (L, CONV_HALO + L), 1)
    shift_sel = jnp.concatenate(
        [jnp.where(sel_c == sel_r + (CONV_HALO - (M_CONV - 1) + j), 1.0, 0.0)
         for j in range(M_CONV - 1)], axis=0).astype(BF16)

    gates = gate_ref[...] + bias_ref[...]
    logf = jax.nn.log_sigmoid(gates)
    row = lax.broadcasted_iota(jnp.int32, (L, L), 0)
    col = lax.broadcasted_iota(jnp.int32, (L, L), 1)
    tril = jnp.where(col <= row, 1.0, 0.0).astype(F32)
    bcum = jnp.dot(tril, logf, precision=lax.Precision.HIGHEST,
                   preferred_element_type=F32)
    gates_t = gates.T
    bcum_t = bcum.T
    key_le_query = row <= col
    sub8 = lax.broadcasted_iota(jnp.int32, (8, 1), 0)

    def hi_lo_rows(r):
        hi = r.astype(BF16).astype(F32)
        return jnp.where(sub8 == 0, hi, jnp.where(sub8 == 1, r - hi, 0.0)).astype(BF16)

    def conv_silu(shifted, c0):
        acc = (hist_ref[CONV_HALO:CONV_HALO + L, c0:c0 + D].astype(F32)
               * conv_ref[M_CONV - 1:M_CONV, c0:c0 + D])
        for j in range(M_CONV - 1):
            acc = acc + shifted[j * L:(j + 1) * L, :] * conv_ref[j:j + 1, c0:c0 + D]
        return acc * jax.nn.sigmoid(acc)

    def head_stages(h):
        c0 = h * D
        shifted = [_dot(shift_sel, hist_ref[:, cc:cc + D])
                   for cc in (c0, M_WIDTH + c0)]
        yield
        qb = conv_silu(shifted[0], c0).astype(BF16)
        kb = (conv_silu(shifted[1], M_WIDTH + c0) * (M_HEAD_DIM ** -0.5)).astype(BF16)
        st = _dot_nt(kb, qb)
        inter_t = _dot_nt(ct_ref[h].astype(BF16), qb)
        qn2 = _dot_nt(hi_lo_rows(n_ref[h:h + 1, :]), qb)
        yield
        b_row = bcum_t[M_HEADS + h:M_HEADS + h + 1, :]
        i_row = gates_t[h:h + 1, :]
        c_col = gates[:, h:h + 1] - bcum[:, M_HEADS + h:M_HEADS + h + 1]
        m_prev = m_ref[h:h + 1, 0:1]
        g_row = b_row + m_prev
        dm_t = jnp.where(key_le_query, c_col + b_row, -jnp.inf)
        mj = jnp.maximum(g_row, jnp.max(dm_t, axis=0, keepdims=True))
        s_t = st * jnp.exp(dm_t - mj)
        vt = vo_ref[:, c0:c0 + D].astype(F32).T
        intra_t = _dot(vt.astype(BF16), s_t.astype(BF16))
        b_last = b_row[:, L - 1:L]
        w_row = b_last - b_row + i_row
        m_new = jnp.maximum(b_last + m_prev, jnp.max(w_row, axis=1, keepdims=True))
        decay = jnp.exp(b_last + m_prev - m_new)
        wi_row = jnp.exp(w_row - m_new)
        ct_new = _dot((vt * wi_row).astype(BF16), kb)
        n_new2 = _dot(hi_lo_rows(wi_row), kb)
        yield
        wg = jnp.exp(g_row - mj)
        num_t = wg * inter_t + intra_t
        den = wg * (qn2[0:1, :] + qn2[1:2, :]) + jnp.sum(s_t, axis=0, keepdims=True)
        hh_t = num_t * (1.0 / jnp.maximum(jnp.abs(den), jnp.exp(-mj)))
        rs = lax.rsqrt(jnp.mean(hh_t * hh_t, axis=0, keepdims=True) + EPS)
        hn_t = hh_t * rs * ngb_ref[c0:c0 + D, :]
        og = vo_ref[:, M_WIDTH + c0:M_WIDTH + c0 + D].astype(F32)
        out_ref[:, c0:c0 + D] = (jax.nn.sigmoid(og) * hn_t.T).astype(out_ref.dtype)
        ct_ref[h] = decay * ct_ref[h] + ct_new
        n_ref[h:h + 1, :] = decay * n_ref[h:h + 1, :] + n_new2[0:1, :] + n_new2[1:2, :]
        m_ref[h:h + 1, :] = jnp.broadcast_to(m_new, (1, LANES))

    pending = [head_stages(h) for h in range(M_HEADS)]
    running = []
    while pending or running:
        if pending:
            running.append(pending.pop(0))
        for gen in list(running):
            if next(gen, "done") == "done":
                running.remove(gen)

    hist_ref[0:CONV_HALO, :] = hist_ref[L:L + CONV_HALO, :]


def _mlstm(qk, vo, gates, conv_qk, gate_bias, norm_g_lanes, batch, seq):
    nc = seq // M_CHUNK
    blk = lambda b, c: (b * nc + c, 0)
    return pl.pallas_call(
        _mlstm_kernel,
        grid=(batch, nc),
        in_specs=[pl.BlockSpec((M_CHUNK, 2 * M_WIDTH), blk),
                  pl.BlockSpec((M_CHUNK, 2 * M_WIDTH), blk),
                  pl.BlockSpec((M_CHUNK, LANES), blk),
                  _const_spec((M_CONV, 2 * M_WIDTH)),
                  _const_spec((1, LANES)),
                  _const_spec((M_WIDTH, LANES))],
        out_specs=pl.BlockSpec((M_CHUNK, M_WIDTH), blk),
        out_shape=jax.ShapeDtypeStruct((batch * seq, M_WIDTH), BF16),
        scratch_shapes=[pltpu.VMEM((M_CHUNK + CONV_HALO, 2 * M_WIDTH), BF16),
                        pltpu.VMEM((M_HEADS, M_HEAD_DIM, M_HEAD_DIM), F32),
                        pltpu.VMEM((8, M_HEAD_DIM), F32),
                        pltpu.VMEM((8, LANES), F32)],
        compiler_params=pltpu.CompilerParams(dimension_semantics=("arbitrary", "arbitrary"),
                                             vmem_limit_bytes=VMEM_LIMIT),
        name="mlstm",
    )(qk, vo, gates, conv_qk, gate_bias, norm_g_lanes)


def _swa_kernel(sink_ref, qa_ref, kva_ref, out_ref, k_ref, vt_ref):
    W = WINDOW
    half = A_HEAD_DIM // 2

    @pl.when(pl.program_id(1) == 0)
    def _():
        for s in range(4):
            k_ref[s * 2 * W:s * 2 * W + W, :] = jnp.zeros((W, LANES), BF16)
        vt_ref[:, 0:W] = jnp.zeros((LANES, W), BF16)

    @pl.when(pl.program_id(1) > 0)
    def _():
        for s in range(4):
            k_ref[s * 2 * W:s * 2 * W + W, :] = k_ref[s * 2 * W + W:(s + 1) * 2 * W, :]
        vt_ref[:, 0:W] = vt_ref[:, W:2 * W]

    lane = lax.broadcasted_iota(jnp.int32, (W, LANES), 1)
    slot_a = (lane % A_HEAD_DIM) < half
    kf = kva_ref[:, 0:LANES].astype(F32)
    variants = (jnp.where(slot_a, kf, 0.0),
                jnp.where(slot_a, 0.0, pltpu.roll(kf, half, 1)),
                jnp.where(slot_a, pltpu.roll(kf, LANES - half, 1), 0.0),
                jnp.where(slot_a, 0.0, kf))
    for s, var in enumerate(variants):
        k_ref[s * 2 * W + W:(s + 1) * 2 * W, :] = var.astype(BF16)
    vt_ref[:, W:2 * W] = kva_ref[:, LANES:2 * LANES].astype(F32).T.astype(BF16)

    kpos = lax.broadcasted_iota(jnp.int32, (2 * W, W), 0)
    qpos = lax.broadcasted_iota(jnp.int32, (2 * W, W), 1)
    diff = W + qpos - kpos
    valid = (diff >= 0) & (diff < W) & ((pl.program_id(1) > 0) | (kpos >= W))
    bias = jnp.where(valid, 0.0, -jnp.inf).astype(F32)

    n_pairs = A_Q_HEADS // 2
    pairs_per_group = n_pairs // A_KV_HEADS

    def scores(pair):
        g = pair // pairs_per_group
        q = qa_ref[:, pair * LANES:(pair + 1) * LANES]
        return _dot_nt(k_ref[g * 4 * W:(g + 1) * 4 * W, :], q)

    pending = [scores(p) for p in range(min(SCORE_LOOKAHEAD, n_pairs))]
    for pair in range(n_pairs):
        g = pair // pairs_per_group
        if pair + SCORE_LOOKAHEAD < n_pairs:
            pending.append(scores(pair + SCORE_LOOKAHEAD))
        st_pair = pending.pop(0)
        pes, inv_dens = [], []
        for slot in range(2):
            sink = sink_ref[2 * pair + slot]
            st = st_pair[slot * 2 * W:(slot + 1) * 2 * W, :] + bias
            mx = jnp.maximum(jnp.max(st, axis=0, keepdims=True), sink)
            pe = jnp.exp(st - mx)
            den = jnp.sum(pe, axis=0, keepdims=True) + jnp.exp(sink - mx)
            pes.append(pe.astype(BF16))
            inv_dens.append(1.0 / den)
        ot = _dot(vt_ref[g * A_HEAD_DIM:(g + 1) * A_HEAD_DIM, :],
                  jnp.concatenate(pes, axis=1))
        o_pair = jnp.concatenate([ot[:, slot * W:(slot + 1) * W] * inv_dens[slot]
                                  for slot in range(2)], axis=0)
        out_ref[:, pair * LANES:(pair + 1) * LANES] = o_pair.T.astype(out_ref.dtype)


def _swa(qa, kva, sinks, batch, seq):
    nb = seq // WINDOW
    blk = lambda b, c: (b * nb + c, 0)
    return pl.pallas_call(
        _swa_kernel,
        grid=(batch, nb),
        in_specs=[pl.BlockSpec(memory_space=pltpu.SMEM),
                  pl.BlockSpec((WINDOW, A_WIDTH), blk),
                  pl.BlockSpec((WINDOW, 2 * A_KV_WIDTH), blk)],
        out_specs=pl.BlockSpec((WINDOW, A_WIDTH), blk),
        out_shape=jax.ShapeDtypeStruct((batch * seq, A_WIDTH), BF16),
        scratch_shapes=[pltpu.VMEM((4 * 2 * WINDOW, LANES), BF16),
                        pltpu.VMEM((LANES, 2 * WINDOW), BF16)],
        compiler_params=pltpu.CompilerParams(dimension_semantics=("arbitrary", "arbitrary"),
                                             vmem_limit_bytes=VMEM_LIMIT),
        name="swa",
    )(sinks, qa, kva)


def _mix_out_kernel(hm_ref, ha_ref, x_ref, wm_ref, wa_ref, gpost_ref, gpre_ref, wq_ref,
                    h1_ref, xq_ref):
    a = _dot(hm_ref[...], wm_ref[...]) + _dot(ha_ref[...], wa_ref[...])
    h1 = x_ref[...] + _rms(a, gpost_ref[...])
    h1_ref[...] = h1
    u = _rms(h1, gpre_ref[...]).astype(BF16)
    xq_ref[...] = (_dot(u, wq_ref[...]) * (X_HEAD_DIM ** -0.5)).astype(xq_ref.dtype)


def _mix_out(hm, ha, x2, w_m, w_a, g_post, g_pre, w_xq, tm):
    t = x2.shape[0]
    row = lambda i: (i, 0)
    return pl.pallas_call(
        _mix_out_kernel,
        grid=(t // tm,),
        in_specs=[pl.BlockSpec((tm, M_WIDTH), row),
                  pl.BlockSpec((tm, A_WIDTH), row),
                  pl.BlockSpec((tm, D_MODEL), row),
                  _const_spec(w_m.shape), _const_spec(w_a.shape),
                  _const_spec((1, D_MODEL)), _const_spec((1, D_MODEL)),
                  _const_spec(w_xq.shape)],
        out_specs=[pl.BlockSpec((tm, D_MODEL), row), pl.BlockSpec((tm, D_MODEL), row)],
        out_shape=[jax.ShapeDtypeStruct((t, D_MODEL), F32),
                   jax.ShapeDtypeStruct((t, D_MODEL), BF16)],
        compiler_params=pltpu.CompilerParams(dimension_semantics=("arbitrary",),
                                             vmem_limit_bytes=VMEM_LIMIT),
        name="mix_out",
    )(hm, ha, x2, w_m, w_a, g_post, g_pre, w_xq)


def _mem_kv_kernel(mem_ref, g_ref, w_ref, kv_ref):
    u = _rms(mem_ref[...], g_ref[...]).astype(BF16)
    kv_ref[...] = _dot(u, w_ref[...]).astype(kv_ref.dtype)


def _mem_kv(mem2, g, w_xkv, mem_len):
    t = mem2.shape[0]
    return pl.pallas_call(
        _mem_kv_kernel,
        grid=(t // mem_len,),
        in_specs=[pl.BlockSpec((mem_len, D_MODEL), lambda i: (i, 0)),
                  _const_spec((1, D_MODEL)),
                  _const_spec(w_xkv.shape)],
        out_specs=pl.BlockSpec((mem_len, 2 * D_MODEL), lambda i: (i, 0)),
        out_shape=jax.ShapeDtypeStruct((t, 2 * D_MODEL), BF16),
        compiler_params=pltpu.CompilerParams(dimension_semantics=("arbitrary",),
                                             vmem_limit_bytes=VMEM_LIMIT),
        name="mem_kv",
    )(mem2, g, w_xkv)


def _xattn_ffn_kernel(h1_ref, xq_ref, kv_ref, wo_ref, gxpost_ref, gfpre_ref, wgu_ref, wd_ref,
                      gfpost_ref, out_ref, o_ref, *, ff_chunk):
    d_ff = wd_ref.shape[0]

    def scores(h):
        c0 = h * X_HEAD_DIM
        return _dot_nt(xq_ref[:, c0:c0 + X_HEAD_DIM], kv_ref[:, c0:c0 + X_HEAD_DIM])

    s = scores(0)
    for h in range(X_HEADS):
        c0 = h * X_HEAD_DIM
        s_next = scores(h + 1) if h + 1 < X_HEADS else None
        e = jnp.exp(s - jnp.max(s, axis=1, keepdims=True))
        inv = 1.0 / jnp.sum(e, axis=1, keepdims=True)
        o = _dot(e.astype(BF16), kv_ref[:, D_MODEL + c0:D_MODEL + c0 + X_HEAD_DIM])
        o_ref[:, c0:c0 + X_HEAD_DIM] = (o * inv).astype(BF16)
        s = s_next
    c = _dot(o_ref[...], wo_ref[...])
    h2 = h1_ref[...] + _rms(c, gxpost_ref[...])
    u = _rms(h2, gfpre_ref[...]).astype(BF16)
    f = None
    for j0 in range(0, d_ff, ff_chunk):
        j1 = min(j0 + ff_chunk, d_ff)
        gate = _dot(u, wgu_ref[:, j0:j1])
        up = _dot(u, wgu_ref[:, d_ff + j0:d_ff + j1])
        act = (gate * jax.nn.sigmoid(gate) * up).astype(BF16)
        part = _dot(act, wd_ref[j0:j1, :])
        f = part if f is None else f + part
    out_ref[...] = h2 + _rms(f, gfpost_ref[...])


def _xattn_ffn(h1, xq, kv, w_xo, g_xpost, g_fpre, w_gu, w_d, g_fpost, tm, seq, mem_len, ff_chunk):
    t = h1.shape[0]
    row = lambda i: (i, 0)
    per_b = seq // tm
    return pl.pallas_call(
        functools.partial(_xattn_ffn_kernel, ff_chunk=ff_chunk),
        grid=(t // tm,),
        in_specs=[pl.BlockSpec((tm, D_MODEL), row),
                  pl.BlockSpec((tm, D_MODEL), row),
                  pl.BlockSpec((mem_len, 2 * D_MODEL), lambda i: (i // per_b, 0)),
                  _const_spec(w_xo.shape),
                  _const_spec((1, D_MODEL)), _const_spec((1, D_MODEL)),
                  _const_spec(w_gu.shape), _const_spec(w_d.shape),
                  _const_spec((1, D_MODEL))],
        out_specs=pl.BlockSpec((tm, D_MODEL), row),
        out_shape=jax.ShapeDtypeStruct((t, D_MODEL), F32),
        scratch_shapes=[pltpu.VMEM((tm, D_MODEL), BF16)],
        compiler_params=pltpu.CompilerParams(dimension_semantics=("arbitrary",),
                                             vmem_limit_bytes=VMEM_LIMIT),
        name="xattn_ffn",
    )(h1, xq, kv, w_xo, g_xpost, g_fpre, w_gu, w_d, g_fpost)


def _layer(h, mem2, pos2, inv_freq, p, batch, seq, mem_len):
    w_in = p["w_in"]
    o_i = 4 * M_WIDTH
    o_qa = o_i + 2 * M_HEADS
    o_ka = o_qa + A_WIDTH
    w_gate = jnp.pad(w_in[:, o_i:o_qa], ((0, 0), (0, LANES - 2 * M_HEADS)))

    def pair_interleave(w):
        w5 = w.reshape(w.shape[0], -1, 2, 2, A_HEAD_DIM // 2)
        return jnp.swapaxes(w5, 2, 3).reshape(w.shape)

    w_cat = jnp.concatenate([w_in[:, :o_i], pair_interleave(w_in[:, o_qa:o_ka]),
                             pair_interleave(w_in[:, o_ka:o_ka + A_KV_WIDTH]),
                             w_in[:, o_ka + A_KV_WIDTH:], w_gate], axis=1).astype(BF16)
    gate_bias = jnp.pad(jnp.concatenate([p["i_bias"], p["f_bias"]]),
                        (0, LANES - 2 * M_HEADS)).reshape(1, LANES)
    row = lambda v: v.reshape(1, -1)

    qk, vo, qa, kva, gates = _in_proj(h, row(p["mix_pre_g"]), pos2, inv_freq, w_cat, tm=512)
    norm_g_lanes = jnp.broadcast_to(p["mlstm_norm_g"][:, None], (M_WIDTH, LANES))
    hm = _mlstm(qk, vo, gates, p["conv_qk"], gate_bias, norm_g_lanes, batch, seq)
    ha = _swa(qa, kva, p["attn_sinks"], batch, seq)
    w_out = p["w_out"].astype(BF16)
    h1, xq = _mix_out(hm, ha, h, w_out[:M_WIDTH], w_out[M_WIDTH:], row(p["mix_post_g"]),
                      row(p["xattn_pre_g"]), p["w_xq"].astype(BF16), tm=512)
    kv = _mem_kv(mem2, row(p["mem_norm_g"]), p["w_xkv"].astype(BF16), mem_len)
    return _xattn_ffn(h1, xq, kv, p["w_xo"].astype(BF16), row(p["xattn_post_g"]),
                      row(p["ffn_pre_g"]), p["w_gate_up"].astype(BF16), p["w_down"].astype(BF16),
                      row(p["ffn_post_g"]), tm=512, seq=seq, mem_len=mem_len, ff_chunk=1536)


def kernel(x, mem, positions, mix_pre_g, mix_post_g, w_in, conv_qk, f_bias, i_bias, mlstm_norm_g,
           attn_sinks, w_out, xattn_pre_g, xattn_post_g, mem_norm_g, w_xq, w_xkv, w_xo,
           ffn_pre_g, ffn_post_g, w_gate_up, w_down):
    batch, seq, _ = x.shape
    mem_len = mem.shape[1]
    depth = w_in.shape[0]
    params = dict(mix_pre_g=mix_pre_g, mix_post_g=mix_post_g, w_in=w_in, conv_qk=conv_qk,
                  f_bias=f_bias, i_bias=i_bias, mlstm_norm_g=mlstm_norm_g, attn_sinks=attn_sinks,
                  w_out=w_out, xattn_pre_g=xattn_pre_g, xattn_post_g=xattn_post_g,
                  mem_norm_g=mem_norm_g, w_xq=w_xq, w_xkv=w_xkv, w_xo=w_xo, ffn_pre_g=ffn_pre_g,
                  ffn_post_g=ffn_post_g, w_gate_up=w_gate_up, w_down=w_down)
    inv = ROPE_THETA ** (-jnp.arange(0, A_HEAD_DIM, 2, dtype=F32) / A_HEAD_DIM)
    inv_freq = jnp.tile(inv, LANES // inv.shape[0]).reshape(1, LANES)
    h = x.reshape(batch * seq, D_MODEL)
    mem2 = mem.reshape(batch * mem_len, D_MODEL)
    pos2 = positions.reshape(batch * seq, 1)
    for l in range(depth):
        h = _layer(h, mem2, pos2, inv_freq, {k: v[l] for k, v in params.items()},
                   batch, seq, mem_len)
    return h.reshape(batch, seq, D_MODEL)
```

```python
import functools

import jax
import jax.numpy as jnp
from jax import lax
from jax.experimental import pallas as pl
from jax.experimental.pallas import tpu as pltpu

D_MODEL = 1024
EPS = 1e-6
M_HEADS = 4
M_HEAD_DIM = 256
M_WIDTH = M_HEADS * M_HEAD_DIM
M_CONV = 4
M_CHUNK = 128
A_Q_HEADS = 16
A_KV_HEADS = 2
A_HEAD_DIM = 64
A_WIDTH = A_Q_HEADS * A_HEAD_DIM
A_KV_WIDTH = A_KV_HEADS * A_HEAD_DIM
WINDOW = 128
ROPE_THETA = 10000.0
LOG2E = 1.4426950408889634
X_HEADS = 4
X_HEAD_DIM = D_MODEL // X_HEADS
LANES = 128
CONV_HALO = 16
ROW_GROUPS = 2
M_ROWS_PER_STEP = 4
SCORE_LOOKAHEAD = 3
VMEM_LIMIT = 56 * 1024 * 1024

F32 = jnp.float32
BF16 = jnp.bfloat16


def _rms(x, g):
    return x * lax.rsqrt(jnp.mean(x * x, axis=-1, keepdims=True) + EPS) * g


def _dot(a, b):
    return jnp.dot(a, b, preferred_element_type=F32)


def _dot_nt(a, b):
    return lax.dot_general(a, b, (((1,), (1,)), ((), ())), preferred_element_type=F32)


def _dot_tn(a, b):
    return lax.dot_general(a, b, (((0,), (0,)), ((), ())), preferred_element_type=F32)


def _run_staggered(chains, start_per_round=1):
    pending = list(chains)
    running = []
    while pending or running:
        for _ in range(min(start_per_round, len(pending))):
            running.append(pending.pop(0))
        for gen in list(running):
            if next(gen, "done") == "done":
                running.remove(gen)


def _const_spec(shape):
    nd = len(shape)
    return pl.BlockSpec(shape, lambda *_: (0,) * nd, pipeline_mode=pl.Buffered(1))


def _in_proj_kernel(x_ref, g_ref, pos_ref, inv_ref, w_ref, qk_ref, vo_ref, qa_ref, kva_ref,
                    gate_ref):
    tm = x_ref.shape[0]
    u = _rms(x_ref[...], g_ref[...]).astype(BF16)
    off = 0
    for ref in (qk_ref, vo_ref):
        width = ref.shape[-1]
        ref[...] = _dot(u, w_ref[:, off:off + width]).astype(ref.dtype)
        off += width

    n_freq = A_HEAD_DIM // 2
    n_seg = LANES // n_freq
    rows = tm // n_seg
    seg = lax.broadcasted_iota(jnp.int32, (rows, LANES), 1) // n_freq
    pos_f = pos_ref[...].astype(F32)
    pos_c = jnp.broadcast_to(pos_f[0:rows], (rows, LANES))
    for s in range(1, n_seg):
        pos_c = jnp.where(seg == s, pos_f[s * rows:(s + 1) * rows], pos_c)
    ang_c = pos_c * inv_ref[...]

    def spread(table_c):
        rolled = [table_c] + [pltpu.roll(table_c, n_freq * k, 1) for k in range(1, n_seg)]
        chunks = []
        for s in range(n_seg):
            full = rolled[(0 - s) % n_seg]
            for p in range(1, n_seg):
                full = jnp.where(seg == p, rolled[(p - s) % n_seg], full)
            chunks.append(full)
        return jnp.concatenate(chunks, axis=0)

    lane = lax.broadcasted_iota(jnp.int32, (tm, LANES), 1)
    cos_t = spread(jnp.cos(ang_c))
    sin_t = spread(jnp.sin(ang_c))
    sin_t = jnp.where(lane < LANES // 2, -sin_t, sin_t)

    def rope(t, c, s):
        return t * c + pltpu.roll(t, LANES // 2, 1) * s

    scale = A_HEAD_DIM ** -0.5 * LOG2E
    cos_q = cos_t * scale
    sin_q = sin_t * scale
    for j in range(A_WIDTH // (2 * LANES)):
        t = _dot(u, w_ref[:, off:off + 2 * LANES])
        for i in range(2):
            c0 = (2 * j + i) * LANES
            qa_ref[:, c0:c0 + LANES] = rope(t[:, i * LANES:(i + 1) * LANES],
                                            cos_q, sin_q).astype(qa_ref.dtype)
        off += 2 * LANES
    kv = _dot(u, w_ref[:, off:off + 2 * LANES])
    kva_ref[:, 0:LANES] = rope(kv[:, 0:LANES], cos_t, sin_t).astype(kva_ref.dtype)
    kva_ref[:, LANES:2 * LANES] = kv[:, LANES:2 * LANES].astype(kva_ref.dtype)
    off += 2 * LANES
    gate_ref[...] = _dot(u, w_ref[:, off:off + LANES])


def _in_proj(x2, g, pos, inv_freq, w_cat, tm):
    t = x2.shape[0]
    widths = (2 * M_WIDTH, 2 * M_WIDTH, A_WIDTH, 2 * A_KV_WIDTH, LANES)
    dtypes = (BF16, BF16, BF16, BF16, F32)
    return pl.pallas_call(
        _in_proj_kernel,
        grid=(t // tm,),
        in_specs=[pl.BlockSpec((tm, D_MODEL), lambda i: (i, 0)),
                  _const_spec((1, D_MODEL)),
                  pl.BlockSpec((tm, 1), lambda i: (i, 0)),
                  _const_spec((1, LANES)),
                  _const_spec(w_cat.shape)],
        out_specs=[pl.BlockSpec((tm, w), lambda i: (i, 0)) for w in widths],
        out_shape=[jax.ShapeDtypeStruct((t, w), d) for w, d in zip(widths, dtypes)],
        compiler_params=pltpu.CompilerParams(dimension_semantics=("arbitrary",),
                                             vmem_limit_bytes=VMEM_LIMIT),
        name="in_proj",
    )(x2, g, pos, inv_freq, w_cat)


def _mlstm_kernel(qk_ref, vo_ref, gate_ref, conv_ref, bias_ref, ngb_ref, out_ref,
                  hist_ref, ct_ref, n_ref, m_ref):
    L = M_CHUNK
    D = M_HEAD_DIM
    n_rows = qk_ref.shape[0]

    @pl.when(pl.program_id(1) == 0)
    def _():
        hist_ref[:, 0:CONV_HALO, :] = jnp.zeros((n_rows, CONV_HALO, 2 * M_WIDTH), BF16)
        ct_ref[...] = jnp.zeros_like(ct_ref)
        n_ref[...] = jnp.zeros_like(n_ref)
        m_ref[...] = jnp.zeros_like(m_ref)

    hist_ref[:, CONV_HALO:CONV_HALO + L, :] = qk_ref[...]

    sel_r = lax.broadcasted_iota(jnp.int32, (L, CONV_HALO + L), 0)
    sel_c = lax.broadcasted_iota(jnp.int32, (L, CONV_HALO + L), 1)
    shift_sel = jnp.concatenate(
        [jnp.where(sel_c == sel_r + (CONV_HALO - (M_CONV - 1) + j), 1.0, 0.0)
         for j in range(M_CONV - 1)], axis=0).astype(BF16)

    row = lax.broadcasted_iota(jnp.int32, (L, L), 0)
    col = lax.broadcasted_iota(jnp.int32, (L, L), 1)
    tril = jnp.where(col <= row, 1.0, 0.0).astype(F32)
    key_le_query = row <= col
    sub8 = lax.broadcasted_iota(jnp.int32, (8, 1), 0)

    def gate_terms(r):
        gates = gate_ref[r] + bias_ref[...]
        bcum = jnp.dot(tril, jax.nn.log_sigmoid(gates), precision=lax.Precision.HIGHEST,
                       preferred_element_type=F32)
        return gates, bcum, gates.T, bcum.T

    def hi_lo_rows(v):
        hi = v.astype(BF16).astype(F32)
        return jnp.where(sub8 == 0, hi, jnp.where(sub8 == 1, v - hi, 0.0)).astype(BF16)

    def conv_silu(r, shifted, c0, scale):
        acc = (hist_ref[r, CONV_HALO:CONV_HALO + L, c0:c0 + D].astype(F32)
               * conv_ref[M_CONV - 1:M_CONV, c0:c0 + D])
        for j in range(M_CONV - 1):
            acc = acc + shifted[j * L:(j + 1) * L, :] * conv_ref[j:j + 1, c0:c0 + D]
        half = acc.astype(BF16) * (0.5 * scale)
        arg = half if scale == 1.0 else half * (1.0 / scale)
        return half * jnp.tanh(arg) + half

    def head_stages(r, h, terms):
        gates, bcum, gates_t, bcum_t = terms
        c0 = h * D
        sr = r * M_HEADS + h
        shifted = [_dot(shift_sel, hist_ref[r, :, cc:cc + D])
                   for cc in (c0, M_WIDTH + c0)]
        yield
        qb = conv_silu(r, shifted[0], c0, 1.0)
        kb = conv_silu(r, shifted[1], M_WIDTH + c0, M_HEAD_DIM ** -0.5)
        st = _dot_nt(kb, qb)
        inter_t = _dot_nt(ct_ref[sr].astype(BF16), qb)
        qn2 = _dot_nt(hi_lo_rows(n_ref[sr:sr + 1, :]), qb)
        yield
        b_row = bcum_t[M_HEADS + h:M_HEADS + h + 1, :]
        i_row = gates_t[h:h + 1, :]
        c_col = gates[:, h:h + 1] - bcum[:, M_HEADS + h:M_HEADS + h + 1]
        m_prev = m_ref[sr:sr + 1, 0:1]
        g_row = b_row + m_prev
        dm_t = jnp.where(key_le_query, c_col + b_row, -jnp.inf)
        mj = jnp.maximum(g_row, jnp.max(dm_t, axis=0, keepdims=True))
        s_t = st * jnp.exp(dm_t - mj)
        vt = vo_ref[r, :, c0:c0 + D].astype(F32).T
        intra_t = _dot(vt.astype(BF16), s_t.astype(BF16))
        b_last = b_row[:, L - 1:L]
        w_row = b_last - b_row + i_row
        m_new = jnp.maximum(b_last + m_prev, jnp.max(w_row, axis=1, keepdims=True))
        decay = jnp.exp(b_last + m_prev - m_new)
        wi_row = jnp.exp(w_row - m_new)
        ct_new = _dot((vt * wi_row).astype(BF16), kb)
        n_new2 = _dot(hi_lo_rows(wi_row), kb)
        yield
        wg = jnp.exp(g_row - mj)
        num_t = wg * inter_t + intra_t
        den = wg * (qn2[0:1, :] + qn2[1:2, :]) + jnp.sum(s_t, axis=0, keepdims=True)
        hh_t = num_t * (1.0 / jnp.maximum(jnp.abs(den), jnp.exp(-mj)))
        rs = lax.rsqrt(jnp.mean(hh_t * hh_t, axis=0, keepdims=True) + EPS)
        hn_t = hh_t * rs * ngb_ref[c0:c0 + D, :]
        og = vo_ref[r, :, M_WIDTH + c0:M_WIDTH + c0 + D]
        sig = jnp.tanh(og * 0.5) * 0.5 + 0.5
        out_ref[r, :, c0:c0 + D] = sig * hn_t.T.astype(BF16)
        ct_ref[sr] = decay * ct_ref[sr] + ct_new
        n_ref[sr:sr + 1, :] = decay * n_ref[sr:sr + 1, :] + n_new2[0:1, :] + n_new2[1:2, :]
        m_ref[sr:sr + 1, :] = jnp.broadcast_to(m_new, (1, LANES))

    chains = []
    for r in range(n_rows):
        terms = gate_terms(r)
        chains += [head_stages(r, h, terms) for h in range(M_HEADS)]
    _run_staggered(chains)

    hist_ref[:, 0:CONV_HALO, :] = hist_ref[:, L:L + CONV_HALO, :]


def _mlstm(qk, vo, gates, conv_qk, gate_bias, norm_g_lanes, batch, seq):
    nc = seq // M_CHUNK
    rows = M_ROWS_PER_STEP
    assert batch % rows == 0
    blk = lambda b, c: (b, c, 0)
    out = pl.pallas_call(
        _mlstm_kernel,
        grid=(batch // rows, nc),
        in_specs=[pl.BlockSpec((rows, M_CHUNK, 2 * M_WIDTH), blk),
                  pl.BlockSpec((rows, M_CHUNK, 2 * M_WIDTH), blk),
                  pl.BlockSpec((rows, M_CHUNK, LANES), blk),
                  _const_spec((M_CONV, 2 * M_WIDTH)),
                  _const_spec((1, LANES)),
                  _const_spec((M_WIDTH, LANES))],
        out_specs=pl.BlockSpec((rows, M_CHUNK, M_WIDTH), blk),
        out_shape=jax.ShapeDtypeStruct((batch, seq, M_WIDTH), BF16),
        scratch_shapes=[pltpu.VMEM((rows, M_CHUNK + CONV_HALO, 2 * M_WIDTH), BF16),
                        pltpu.VMEM((rows * M_HEADS, M_HEAD_DIM, M_HEAD_DIM), F32),
                        pltpu.VMEM((rows * M_HEADS, M_HEAD_DIM), F32),
                        pltpu.VMEM((rows * M_HEADS, LANES), F32)],
        compiler_params=pltpu.CompilerParams(dimension_semantics=("arbitrary", "arbitrary"),
                                             vmem_limit_bytes=VMEM_LIMIT),
        name="mlstm",
    )(qk.reshape(batch, seq, -1), vo.reshape(batch, seq, -1), gates.reshape(batch, seq, -1),
      conv_qk, gate_bias, norm_g_lanes)
    return out.reshape(batch * seq, M_WIDTH)


def _swa_kernel(sink_ref, qa_ref, kva_ref, out_ref, k_ref, vt_ref):
    W = WINDOW
    half = A_HEAD_DIM // 2

    @pl.when(pl.program_id(1) == 0)
    def _():
        for s in range(4):
            k_ref[s * 2 * W:s * 2 * W + W, :] = jnp.zeros((W, LANES), BF16)
        vt_ref[:, 0:W] = jnp.zeros((LANES, W), BF16)

    @pl.when(pl.program_id(1) > 0)
    def _():
        for s in range(4):
            k_ref[s * 2 * W:s * 2 * W + W, :] = k_ref[s * 2 * W + W:(s + 1) * 2 * W, :]
        vt_ref[:, 0:W] = vt_ref[:, W:2 * W]

    lane = lax.broadcasted_iota(jnp.int32, (W, LANES), 1)
    slot_a = (lane % A_HEAD_DIM) < half
    kf = kva_ref[:, 0:LANES].astype(F32)
    variants = (jnp.where(slot_a, kf, 0.0),
                jnp.where(slot_a, 0.0, pltpu.roll(kf, half, 1)),
                jnp.where(slot_a, pltpu.roll(kf, LANES - half, 1), 0.0),
                jnp.where(slot_a, 0.0, kf))
    for s, var in enumerate(variants):
        k_ref[s * 2 * W + W:(s + 1) * 2 * W, :] = var.astype(BF16)
    vt_ref[:, W:2 * W] = kva_ref[:, LANES:2 * LANES].astype(F32).T.astype(BF16)

    kpos = lax.broadcasted_iota(jnp.int32, (2 * W, W), 0)
    qpos = lax.broadcasted_iota(jnp.int32, (2 * W, W), 1)
    diff = W + qpos - kpos
    valid = (diff >= 0) & (diff < W) & ((pl.program_id(1) > 0) | (kpos >= W))
    bias = jnp.where(valid, 0.0, -jnp.inf).astype(F32)

    n_pairs = A_Q_HEADS // 2
    pairs_per_group = n_pairs // A_KV_HEADS

    def scores(pair):
        g = pair // pairs_per_group
        q = qa_ref[:, pair * LANES:(pair + 1) * LANES]
        return _dot_nt(k_ref[g * 4 * W:(g + 1) * 4 * W, :], q)

    pending = [scores(p) for p in range(min(SCORE_LOOKAHEAD, n_pairs))]
    for pair in range(n_pairs):
        g = pair // pairs_per_group
        if pair + SCORE_LOOKAHEAD < n_pairs:
            pending.append(scores(pair + SCORE_LOOKAHEAD))
        st_pair = pending.pop(0)
        pes, inv_dens = [], []
        for slot in range(2):
            sink = sink_ref[2 * pair + slot] * LOG2E
            st = st_pair[slot * 2 * W:(slot + 1) * 2 * W, :] + bias
            mx = jnp.maximum(jnp.max(st, axis=0, keepdims=True), sink)
            pe = jnp.exp2(st - mx)
            den = jnp.sum(pe, axis=0, keepdims=True) + jnp.exp2(sink - mx)
            pes.append(pe.astype(BF16))
            inv_dens.append(1.0 / den)
        ot = _dot(vt_ref[g * A_HEAD_DIM:(g + 1) * A_HEAD_DIM, :],
                  jnp.concatenate(pes, axis=1))
        o_pair = jnp.concatenate([ot[:, slot * W:(slot + 1) * W] * inv_dens[slot]
                                  for slot in range(2)], axis=0)
        out_ref[:, pair * LANES:(pair + 1) * LANES] = o_pair.T.astype(out_ref.dtype)


def _swa(qa, kva, sinks, batch, seq):
    nb = seq // WINDOW
    blk = lambda b, c: (b * nb + c, 0)
    return pl.pallas_call(
        _swa_kernel,
        grid=(batch, nb),
        in_specs=[pl.BlockSpec(memory_space=pltpu.SMEM),
                  pl.BlockSpec((WINDOW, A_WIDTH), blk),
                  pl.BlockSpec((WINDOW, 2 * A_KV_WIDTH), blk)],
        out_specs=pl.BlockSpec((WINDOW, A_WIDTH), blk),
        out_shape=jax.ShapeDtypeStruct((batch * seq, A_WIDTH), BF16),
        scratch_shapes=[pltpu.VMEM((4 * 2 * WINDOW, LANES), BF16),
                        pltpu.VMEM((LANES, 2 * WINDOW), BF16)],
        compiler_params=pltpu.CompilerParams(dimension_semantics=("arbitrary", "arbitrary"),
                                             vmem_limit_bytes=VMEM_LIMIT),
        name="swa",
    )(sinks, qa, kva)


def _mix_out_kernel(hm_ref, ha_ref, x_ref, wm_ref, wa_ref, gpost_ref, gpre_ref, wq_ref,
                    h1_ref, xq_ref):
    def rows_stages(r0, r1):
        a = _dot(hm_ref[r0:r1, :], wm_ref[...]) + _dot(ha_ref[r0:r1, :], wa_ref[...])
        yield
        h1 = x_ref[r0:r1, :] + _rms(a, gpost_ref[...])
        h1_ref[r0:r1, :] = h1
        u = _rms(h1, gpre_ref[...]).astype(BF16)
        xq_ref[r0:r1, :] = (_dot(u, wq_ref[...])
                            * (X_HEAD_DIM ** -0.5 * LOG2E)).astype(xq_ref.dtype)

    tm = x_ref.shape[0]
    step = tm // ROW_GROUPS
    _run_staggered([rows_stages(r0, r0 + step) for r0 in range(0, tm, step)],
                   start_per_round=ROW_GROUPS)


def _mix_out(hm, ha, x2, w_m, w_a, g_post, g_pre, w_xq, tm):
    t = x2.shape[0]
    row = lambda i: (i, 0)
    return pl.pallas_call(
        _mix_out_kernel,
        grid=(t // tm,),
        in_specs=[pl.BlockSpec((tm, M_WIDTH), row),
                  pl.BlockSpec((tm, A_WIDTH), row),
                  pl.BlockSpec((tm, D_MODEL), row),
                  _const_spec(w_m.shape), _const_spec(w_a.shape),
                  _const_spec((1, D_MODEL)), _const_spec((1, D_MODEL)),
                  _const_spec(w_xq.shape)],
        out_specs=[pl.BlockSpec((tm, D_MODEL), row), pl.BlockSpec((tm, D_MODEL), row)],
        out_shape=[jax.ShapeDtypeStruct((t, D_MODEL), F32),
                   jax.ShapeDtypeStruct((t, D_MODEL), BF16)],
        compiler_params=pltpu.CompilerParams(dimension_semantics=("arbitrary",),
                                             vmem_limit_bytes=VMEM_LIMIT),
        name="mix_out",
    )(hm, ha, x2, w_m, w_a, g_post, g_pre, w_xq)


def _mem_kv_kernel(mem_ref, g_ref, w_ref, kv_ref):
    u = _rms(mem_ref[...], g_ref[...]).astype(BF16)
    kv_ref[...] = _dot(u, w_ref[...]).astype(kv_ref.dtype)


def _mem_kv(mem2, g, w_xkv, mem_len):
    t = mem2.shape[0]
    return pl.pallas_call(
        _mem_kv_kernel,
        grid=(t // mem_len,),
        in_specs=[pl.BlockSpec((mem_len, D_MODEL), lambda i: (i, 0)),
                  _const_spec((1, D_MODEL)),
                  _const_spec(w_xkv.shape)],
        out_specs=pl.BlockSpec((mem_len, 2 * D_MODEL), lambda i: (i, 0)),
        out_shape=jax.ShapeDtypeStruct((t, 2 * D_MODEL), BF16),
        compiler_params=pltpu.CompilerParams(dimension_semantics=("arbitrary",),
                                             vmem_limit_bytes=VMEM_LIMIT),
        name="mem_kv",
    )(mem2, g, w_xkv)


def _xattn_ffn_kernel(h1_ref, xq_ref, kv_ref, wo_ref, gxpost_ref, gfpre_ref, wgu_ref, wd_ref,
                      gfpost_ref, out_ref, o_ref, *, ff_chunk):
    d_ff = wd_ref.shape[0]
    chunks = [(j0, min(j0 + ff_chunk, d_ff)) for j0 in range(0, d_ff, ff_chunk)]

    def rows_stages(r0, r1):
        heads = [h * X_HEAD_DIM for h in range(X_HEADS)]
        scores = [_dot_nt(xq_ref[r0:r1, c0:c0 + X_HEAD_DIM], kv_ref[:, c0:c0 + X_HEAD_DIM])
                  for c0 in heads]
        yield
        for c0, s in zip(heads, scores):
            e = jnp.exp2(s - jnp.max(s, axis=1, keepdims=True))
            inv = 1.0 / jnp.sum(e, axis=1, keepdims=True)
            o = _dot(e.astype(BF16), kv_ref[:, D_MODEL + c0:D_MODEL + c0 + X_HEAD_DIM])
            o_ref[r0:r1, c0:c0 + X_HEAD_DIM] = (o * inv).astype(BF16)
        yield
        c = _dot(o_ref[r0:r1, :], wo_ref[...])
        yield
        h2 = h1_ref[r0:r1, :] + _rms(c, gxpost_ref[...])
        u = _rms(h2, gfpre_ref[...]).astype(BF16)
        f = None
        for j0, j1 in chunks:
            gate = _dot(u, wgu_ref[:, j0:j1])
            up = _dot(u, wgu_ref[:, d_ff + j0:d_ff + j1])
            yield
            act = (gate * jax.nn.sigmoid(gate) * up).astype(BF16)
            part = _dot(act, wd_ref[j0:j1, :])
            f = part if f is None else f + part
        yield
        out_ref[r0:r1, :] = h2 + _rms(f, gfpost_ref[...])

    tm = h1_ref.shape[0]
    step = tm // ROW_GROUPS
    _run_staggered([rows_stages(r0, r0 + step) for r0 in range(0, tm, step)],
                   start_per_round=ROW_GROUPS)


def _xattn_ffn(h1, xq, kv, w_xo, g_xpost, g_fpre, w_gu, w_d, g_fpost, tm, seq, mem_len, ff_chunk):
    t = h1.shape[0]
    row = lambda i: (i, 0)
    per_b = seq // tm
    return pl.pallas_call(
        functools.partial(_xattn_ffn_kernel, ff_chunk=ff_chunk),
        grid=(t // tm,),
        in_specs=[pl.BlockSpec((tm, D_MODEL), row),
                  pl.BlockSpec((tm, D_MODEL), row),
                  pl.BlockSpec((mem_len, 2 * D_MODEL), lambda i: (i // per_b, 0)),
                  _const_spec(w_xo.shape),
                  _const_spec((1, D_MODEL)), _const_spec((1, D_MODEL)),
                  _const_spec(w_gu.shape), _const_spec(w_d.shape),
                  _const_spec((1, D_MODEL))],
        out_specs=pl.BlockSpec((tm, D_MODEL), row),
        out_shape=jax.ShapeDtypeStruct((t, D_MODEL), F32),
        scratch_shapes=[pltpu.VMEM((tm, D_MODEL), BF16)],
        compiler_params=pltpu.CompilerParams(dimension_semantics=("arbitrary",),
                                             vmem_limit_bytes=VMEM_LIMIT),
        name="xattn_ffn",
    )(h1, xq, kv, w_xo, g_xpost, g_fpre, w_gu, w_d, g_fpost)


def _layer(h, mem2, pos2, inv_freq, p, batch, seq, mem_len):
    w_in = p["w_in"]
    o_i = 4 * M_WIDTH
    o_qa = o_i + 2 * M_HEADS
    o_ka = o_qa + A_WIDTH
    w_gate = jnp.pad(w_in[:, o_i:o_qa], ((0, 0), (0, LANES - 2 * M_HEADS)))

    def pair_interleave(w):
        w5 = w.reshape(w.shape[0], -1, 2, 2, A_HEAD_DIM // 2)
        return jnp.swapaxes(w5, 2, 3).reshape(w.shape)

    w_cat = jnp.concatenate([w_in[:, :o_i], pair_interleave(w_in[:, o_qa:o_ka]),
                             pair_interleave(w_in[:, o_ka:o_ka + A_KV_WIDTH]),
                             w_in[:, o_ka + A_KV_WIDTH:], w_gate], axis=1).astype(BF16)
    gate_bias = jnp.pad(jnp.concatenate([p["i_bias"], p["f_bias"]]),
                        (0, LANES - 2 * M_HEADS)).reshape(1, LANES)
    row = lambda v: v.reshape(1, -1)

    qk, vo, qa, kva, gates = _in_proj(h, row(p["mix_pre_g"]), pos2, inv_freq, w_cat, tm=512)
    norm_g_lanes = jnp.broadcast_to(p["mlstm_norm_g"][:, None], (M_WIDTH, LANES))
    hm = _mlstm(qk, vo, gates, p["conv_qk"], gate_bias, norm_g_lanes, batch, seq)
    ha = _swa(qa, kva, p["attn_sinks"], batch, seq)
    w_out = p["w_out"].astype(BF16)
    h1, xq = _mix_out(hm, ha, h, w_out[:M_WIDTH], w_out[M_WIDTH:], row(p["mix_post_g"]),
                      row(p["xattn_pre_g"]), p["w_xq"].astype(BF16), tm=1024)
    kv = _mem_kv(mem2, row(p["mem_norm_g"]), p["w_xkv"].astype(BF16), mem_len)
    return _xattn_ffn(h1, xq, kv, p["w_xo"].astype(BF16), row(p["xattn_post_g"]),
                      row(p["ffn_pre_g"]), p["w_gate_up"].astype(BF16), p["w_down"].astype(BF16),
                      row(p["ffn_post_g"]), tm=512, seq=seq, mem_len=mem_len, ff_chunk=1536)


def kernel(x, mem, positions, mix_pre_g, mix_post_g, w_in, conv_qk, f_bias, i_bias, mlstm_norm_g,
           attn_sinks, w_out, xattn_pre_g, xattn_post_g, mem_norm_g, w_xq, w_xkv, w_xo,
           ffn_pre_g, ffn_post_g, w_gate_up, w_down):
    batch, seq, _ = x.shape
    mem_len = mem.shape[1]
    depth = w_in.shape[0]
    params = dict(mix_pre_g=mix_pre_g, mix_post_g=mix_post_g, w_in=w_in, conv_qk=conv_qk,
                  f_bias=f_bias, i_bias=i_bias, mlstm_norm_g=mlstm_norm_g, attn_sinks=attn_sinks,
                  w_out=w_out, xattn_pre_g=xattn_pre_g, xattn_post_g=xattn_post_g,
                  mem_norm_g=mem_norm_g, w_xq=w_xq, w_xkv=w_xkv, w_xo=w_xo, ffn_pre_g=ffn_pre_g,
                  ffn_post_g=ffn_post_g, w_gate_up=w_gate_up, w_down=w_down)
    inv = ROPE_THETA ** (-jnp.arange(0, A_HEAD_DIM, 2, dtype=F32) / A_HEAD_DIM)
    inv_freq = jnp.tile(inv, LANES // inv.shape[0]).reshape(1, LANES)
    h = x.reshape(batch * seq, D_MODEL)
    mem2 = mem.reshape(batch * mem_len, D_MODEL)
    pos2 = positions.reshape(batch * seq, 1)
    for l in range(depth):
        h = _layer(h, mem2, pos2, inv_freq, {k: v[l] for k, v in params.items()},
                   batch, seq, mem_len)
    return h.reshape(batch, seq, D_MODEL)
```

```python
import functools

import jax
import jax.numpy as jnp
from jax import lax
from jax.experimental import pallas as pl
from jax.experimental.pallas import tpu as pltpu

D_MODEL = 1024
EPS = 1e-6
M_HEADS = 4
M_HEAD_DIM = 256
M_WIDTH = M_HEADS * M_HEAD_DIM
M_CONV = 4
M_CHUNK = 128
A_Q_HEADS = 16
A_KV_HEADS = 2
A_HEAD_DIM = 64
A_WIDTH = A_Q_HEADS * A_HEAD_DIM
A_KV_WIDTH = A_KV_HEADS * A_HEAD_DIM
WINDOW = 128
ROPE_THETA = 10000.0
LOG2E = 1.4426950408889634
X_HEADS = 4
X_HEAD_DIM = D_MODEL // X_HEADS
LANES = 128
CONV_HALO = 16
ROW_GROUPS = 2
M_ROWS_PER_STEP = 4
SCORE_LOOKAHEAD = 3
EARLY_STREAM_CHUNKS = 4
VMEM_LIMIT = 56 * 1024 * 1024

F32 = jnp.float32
BF16 = jnp.bfloat16


def _rms(x, g):
    return x * lax.rsqrt(jnp.mean(x * x, axis=-1, keepdims=True) + EPS) * g


def _dot(a, b):
    return jnp.dot(a, b, preferred_element_type=F32)


def _dot_nt(a, b):
    return lax.dot_general(a, b, (((1,), (1,)), ((), ())), preferred_element_type=F32)


def _dot_tn(a, b):
    return lax.dot_general(a, b, (((0,), (0,)), ((), ())), preferred_element_type=F32)


def _run_staggered(chains, start_per_round=1):
    pending = list(chains)
    running = []
    while pending or running:
        for _ in range(min(start_per_round, len(pending))):
            running.append(pending.pop(0))
        for gen in list(running):
            if next(gen, "done") == "done":
                running.remove(gen)


def _const_spec(shape):
    nd = len(shape)
    return pl.BlockSpec(shape, lambda *_: (0,) * nd, pipeline_mode=pl.Buffered(1))


def _in_proj_kernel(sink_ref, x_ref, g_ref, pos_ref, inv_ref, w_ref, qk_ref, vo_ref, ha_ref,
                    gate_ref, q_scr, k_ref, vt_ref, *, tiles_per_seq):
    W = WINDOW
    tm = x_ref.shape[0]
    n_blk = tm // W
    half = A_HEAD_DIM // 2
    first_tile = pl.program_id(0) % tiles_per_seq == 0

    @pl.when(first_tile)
    def _():
        for s in range(4):
            k_ref[0, s * 2 * W:s * 2 * W + W, :] = jnp.zeros((W, LANES), BF16)
        vt_ref[0, :, 0:W] = jnp.zeros((LANES, W), BF16)

    @pl.when(jnp.logical_not(first_tile))
    def _():
        for s in range(4):
            k_ref[0, s * 2 * W:s * 2 * W + W, :] = k_ref[n_blk, s * 2 * W:s * 2 * W + W, :]
        vt_ref[0, :, 0:W] = vt_ref[n_blk, :, 0:W]

    u = _rms(x_ref[...], g_ref[...]).astype(BF16)

    stream_chunks = [(ref, c0, base + c0)
                     for ref, base in ((qk_ref, 0), (vo_ref, 2 * M_WIDTH))
                     for c0 in range(0, 2 * M_WIDTH, 2 * LANES)]

    def project_chunk():
        ref, c0, wc = stream_chunks.pop(0)
        ref[:, c0:c0 + 2 * LANES] = _dot(u, w_ref[:, wc:wc + 2 * LANES]).astype(ref.dtype)

    for _ in range(EARLY_STREAM_CHUNKS):
        project_chunk()
    off = 4 * M_WIDTH

    n_freq = A_HEAD_DIM // 2
    n_seg = LANES // n_freq
    rows = tm // n_seg
    seg = lax.broadcasted_iota(jnp.int32, (rows, LANES), 1) // n_freq
    pos_f = pos_ref[...].astype(F32)
    pos_c = jnp.broadcast_to(pos_f[0:rows], (rows, LANES))
    for s in range(1, n_seg):
        pos_c = jnp.where(seg == s, pos_f[s * rows:(s + 1) * rows], pos_c)
    ang_c = pos_c * inv_ref[...]

    def spread(table_c):
        rolled = [table_c] + [pltpu.roll(table_c, n_freq * k, 1) for k in range(1, n_seg)]
        chunks = []
        for s in range(n_seg):
            full = rolled[(0 - s) % n_seg]
            for p in range(1, n_seg):
                full = jnp.where(seg == p, rolled[(p - s) % n_seg], full)
            chunks.append(full)
        return jnp.concatenate(chunks, axis=0)

    lane = lax.broadcasted_iota(jnp.int32, (tm, LANES), 1)
    cos_t = spread(jnp.cos(ang_c))
    sin_t = spread(jnp.sin(ang_c))
    sin_t = jnp.where(lane < LANES // 2, -sin_t, sin_t)

    def rope(t, c, s):
        return t * c + pltpu.roll(t, LANES // 2, 1) * s

    scale = A_HEAD_DIM ** -0.5 * LOG2E
    cos_q = cos_t * scale
    sin_q = sin_t * scale
    o_kv = off + A_WIDTH
    kv = _dot(u, w_ref[:, o_kv:o_kv + 2 * LANES])
    k_rot = rope(kv[:, 0:LANES], cos_t, sin_t)
    slot_a = (lane % A_HEAD_DIM) < half
    variants = (jnp.where(slot_a, k_rot, 0.0),
                jnp.where(slot_a, 0.0, pltpu.roll(k_rot, half, 1)),
                jnp.where(slot_a, pltpu.roll(k_rot, LANES - half, 1), 0.0),
                jnp.where(slot_a, 0.0, k_rot))
    for j in range(n_blk):
        for s, var in enumerate(variants):
            blk = var[j * W:(j + 1) * W, :].astype(BF16)
            k_ref[j, s * 2 * W + W:(s + 1) * 2 * W, :] = blk
            k_ref[j + 1, s * 2 * W:s * 2 * W + W, :] = blk
        vt = kv[j * W:(j + 1) * W, LANES:2 * LANES].T.astype(BF16)
        vt_ref[j, :, W:2 * W] = vt
        vt_ref[j + 1, :, 0:W] = vt

    for j in range(A_WIDTH // (2 * LANES)):
        t = _dot(u, w_ref[:, off:off + 2 * LANES])
        for i in range(2):
            c0 = (2 * j + i) * LANES
            q_scr[:, c0:c0 + LANES] = rope(t[:, i * LANES:(i + 1) * LANES],
                                           cos_q, sin_q).astype(q_scr.dtype)
        off += 2 * LANES
    gate_ref[...] = _dot(u, w_ref[:, o_kv + 2 * LANES:o_kv + 3 * LANES])

    kpos = lax.broadcasted_iota(jnp.int32, (2 * W, W), 0)
    qpos = lax.broadcasted_iota(jnp.int32, (2 * W, W), 1)
    diff = W + qpos - kpos
    band = (diff >= 0) & (diff < W)
    bias_rest = jnp.where(band, 0.0, -jnp.inf).astype(F32)
    bias_first = jnp.where(band & (jnp.logical_not(first_tile) | (kpos >= W)),
                           0.0, -jnp.inf).astype(F32)

    n_pairs = A_Q_HEADS // 2
    pairs_per_group = n_pairs // A_KV_HEADS
    items = [(j, pair) for j in range(n_blk) for pair in range(n_pairs)]

    def scores(j, pair):
        g = pair // pairs_per_group
        q = q_scr[j * W:(j + 1) * W, pair * LANES:(pair + 1) * LANES]
        return _dot_nt(k_ref[j, g * 4 * W:(g + 1) * 4 * W, :], q)

    n_late = len(stream_chunks)
    pending = [scores(*it) for it in items[:SCORE_LOOKAHEAD]]
    for idx, (j, pair) in enumerate(items):
        g = pair // pairs_per_group
        if idx + SCORE_LOOKAHEAD < len(items):
            pending.append(scores(*items[idx + SCORE_LOOKAHEAD]))
        st_pair = pending.pop(0)
        bias = bias_first if j == 0 else bias_rest
        pes, inv_dens = [], []
        for slot in range(2):
            sink = sink_ref[2 * pair + slot] * LOG2E
            st = st_pair[slot * 2 * W:(slot + 1) * 2 * W, :] + bias
            mx = jnp.maximum(jnp.max(st, axis=0, keepdims=True), sink)
            pe = jnp.exp2(st - mx)
            den = jnp.sum(pe, axis=0, keepdims=True) + jnp.exp2(sink - mx)
            pes.append(pe.astype(BF16))
            inv_dens.append(1.0 / den)
        ot = _dot(vt_ref[j, g * A_HEAD_DIM:(g + 1) * A_HEAD_DIM, :],
                  jnp.concatenate(pes, axis=1))
        o_pair = jnp.concatenate([ot[:, slot * W:(slot + 1) * W] * inv_dens[slot]
                                  for slot in range(2)], axis=0)
        ha_ref[j * W:(j + 1) * W, pair * LANES:(pair + 1) * LANES] = (
            o_pair.T.astype(ha_ref.dtype))
        if (idx + 1) * n_late // len(items) > idx * n_late // len(items):
            project_chunk()
    assert not stream_chunks


def _in_proj(sinks, x2, g, pos, inv_freq, w_cat, tm, seq):
    t = x2.shape[0]
    widths = (2 * M_WIDTH, 2 * M_WIDTH, A_WIDTH, LANES)
    dtypes = (BF16, BF16, BF16, F32)
    n_blk = tm // WINDOW
    return pl.pallas_call(
        functools.partial(_in_proj_kernel, tiles_per_seq=seq // tm),
        grid=(t // tm,),
        in_specs=[pl.BlockSpec(memory_space=pltpu.SMEM),
                  pl.BlockSpec((tm, D_MODEL), lambda i: (i, 0)),
                  _const_spec((1, D_MODEL)),
                  pl.BlockSpec((tm, 1), lambda i: (i, 0)),
                  _const_spec((1, LANES)),
                  _const_spec(w_cat.shape)],
        out_specs=[pl.BlockSpec((tm, w), lambda i: (i, 0)) for w in widths],
        out_shape=[jax.ShapeDtypeStruct((t, w), d) for w, d in zip(widths, dtypes)],
        scratch_shapes=[pltpu.VMEM((tm, A_WIDTH), BF16),
                        pltpu.VMEM((n_blk + 1, 4 * 2 * WINDOW, LANES), BF16),
                        pltpu.VMEM((n_blk + 1, LANES, 2 * WINDOW), BF16)],
        compiler_params=pltpu.CompilerParams(dimension_semantics=("arbitrary",),
                                             vmem_limit_bytes=VMEM_LIMIT),
        name="in_proj",
    )(sinks, x2, g, pos, inv_freq, w_cat)


def _mlstm_kernel(qk_ref, vo_ref, gate_ref, conv_ref, bias_ref, ngb_ref, out_ref,
                  hist_ref, ct_ref, n_ref, m_ref):
    L = M_CHUNK
    D = M_HEAD_DIM
    n_rows = qk_ref.shape[0]

    @pl.when(pl.program_id(1) == 0)
    def _():
        hist_ref[:, 0:CONV_HALO, :] = jnp.zeros((n_rows, CONV_HALO, 2 * M_WIDTH), BF16)
        ct_ref[...] = jnp.zeros_like(ct_ref)
        n_ref[...] = jnp.zeros_like(n_ref)
        m_ref[...] = jnp.zeros_like(m_ref)

    hist_ref[:, CONV_HALO:CONV_HALO + L, :] = qk_ref[...]

    sel_r = lax.broadcasted_iota(jnp.int32, (L, CONV_HALO + L), 0)
    sel_c = lax.broadcasted_iota(jnp.int32, (L, CONV_HALO + L), 1)
    shift_sel = jnp.concatenate(
        [jnp.where(sel_c == sel_r + (CONV_HALO - (M_CONV - 1) + j), 1.0, 0.0)
         for j in range(M_CONV - 1)], axis=0).astype(BF16)

    row = lax.broadcasted_iota(jnp.int32, (L, L), 0)
    col = lax.broadcasted_iota(jnp.int32, (L, L), 1)
    tril = jnp.where(col <= row, 1.0, 0.0).astype(F32)
    key_le_query = row <= col
    sub8 = lax.broadcasted_iota(jnp.int32, (8, 1), 0)

    def gate_terms(r):
        gates = gate_ref[r] + bias_ref[...]
        bcum = jnp.dot(tril, jax.nn.log_sigmoid(gates), precision=lax.Precision.HIGHEST,
                       preferred_element_type=F32)
        return gates, bcum, gates.T, bcum.T

    def hi_lo_rows(v):
        hi = v.astype(BF16).astype(F32)
        return jnp.where(sub8 == 0, hi, jnp.where(sub8 == 1, v - hi, 0.0)).astype(BF16)

    def conv_silu(r, shifted, c0, scale):
        acc = (hist_ref[r, CONV_HALO:CONV_HALO + L, c0:c0 + D].astype(F32)
               * conv_ref[M_CONV - 1:M_CONV, c0:c0 + D])
        for j in range(M_CONV - 1):
            acc = acc + shifted[j * L:(j + 1) * L, :] * conv_ref[j:j + 1, c0:c0 + D]
        half = acc.astype(BF16) * (0.5 * scale)
        arg = half if scale == 1.0 else half * (1.0 / scale)
        return half * jnp.tanh(arg) + half

    def head_stages(r, h, terms):
        gates, bcum, gates_t, bcum_t = terms
        c0 = h * D
        sr = r * M_HEADS + h
        shifted = [_dot(shift_sel, hist_ref[r, :, cc:cc + D])
                   for cc in (c0, M_WIDTH + c0)]
        yield
        qb = conv_silu(r, shifted[0], c0, 1.0)
        kb = conv_silu(r, shifted[1], M_WIDTH + c0, M_HEAD_DIM ** -0.5)
        st = _dot_nt(kb, qb)
        inter_t = _dot_nt(ct_ref[sr].astype(BF16), qb)
        qn2 = _dot_nt(hi_lo_rows(n_ref[sr:sr + 1, :]), qb)
        yield
        b_row = bcum_t[M_HEADS + h:M_HEADS + h + 1, :]
        i_row = gates_t[h:h + 1, :]
        c_col = gates[:, h:h + 1] - bcum[:, M_HEADS + h:M_HEADS + h + 1]
        m_prev = m_ref[sr:sr + 1, 0:1]
        g_row = b_row + m_prev
        dm_t = jnp.where(key_le_query, c_col + b_row, -jnp.inf)
        mj = jnp.maximum(g_row, jnp.max(dm_t, axis=0, keepdims=True))
        s_t = st * jnp.exp(dm_t - mj)
        vt = vo_ref[r, :, c0:c0 + D].astype(F32).T
        intra_t = _dot(vt.astype(BF16), s_t.astype(BF16))
        b_last = b_row[:, L - 1:L]
        w_row = b_last - b_row + i_row
        m_new = jnp.maximum(b_last + m_prev, jnp.max(w_row, axis=1, keepdims=True))
        decay = jnp.exp(b_last + m_prev - m_new)
        wi_row = jnp.exp(w_row - m_new)
        ct_new = _dot((vt * wi_row).astype(BF16), kb)
        n_new2 = _dot(hi_lo_rows(wi_row), kb)
        yield
        wg = jnp.exp(g_row - mj)
        num_t = wg * inter_t + intra_t
        den = wg * (qn2[0:1, :] + qn2[1:2, :]) + jnp.sum(s_t, axis=0, keepdims=True)
        hh_t = num_t * (1.0 / jnp.maximum(jnp.abs(den), jnp.exp(-mj)))
        rs = lax.rsqrt(jnp.mean(hh_t * hh_t, axis=0, keepdims=True) + EPS)
        hn_t = hh_t * rs * ngb_ref[c0:c0 + D, :]
        og = vo_ref[r, :, M_WIDTH + c0:M_WIDTH + c0 + D]
        sig = jnp.tanh(og * 0.5) * 0.5 + 0.5
        out_ref[r, :, c0:c0 + D] = sig * hn_t.T.astype(BF16)
        ct_ref[sr] = decay * ct_ref[sr] + ct_new
        n_ref[sr:sr + 1, :] = decay * n_ref[sr:sr + 1, :] + n_new2[0:1, :] + n_new2[1:2, :]
        m_ref[sr:sr + 1, :] = jnp.broadcast_to(m_new, (1, LANES))

    chains = []
    for r in range(n_rows):
        terms = gate_terms(r)
        chains += [head_stages(r, h, terms) for h in range(M_HEADS)]
    _run_staggered(chains)

    hist_ref[:, 0:CONV_HALO, :] = hist_ref[:, L:L + CONV_HALO, :]


def _mlstm(qk, vo, gates, conv_qk, gate_bias, norm_g_lanes, batch, seq):
    nc = seq // M_CHUNK
    rows = M_ROWS_PER_STEP
    assert batch % rows == 0
    blk = lambda b, c: (b, c, 0)
    out = pl.pallas_call(
        _mlstm_kernel,
        grid=(batch // rows, nc),
        in_specs=[pl.BlockSpec((rows, M_CHUNK, 2 * M_WIDTH), blk),
                  pl.BlockSpec((rows, M_CHUNK, 2 * M_WIDTH), blk),
                  pl.BlockSpec((rows, M_CHUNK, LANES), blk),
                  _const_spec((M_CONV, 2 * M_WIDTH)),
                  _const_spec((1, LANES)),
                  _const_spec((M_WIDTH, LANES))],
        out_specs=pl.BlockSpec((rows, M_CHUNK, M_WIDTH), blk),
        out_shape=jax.ShapeDtypeStruct((batch, seq, M_WIDTH), BF16),
        scratch_shapes=[pltpu.VMEM((rows, M_CHUNK + CONV_HALO, 2 * M_WIDTH), BF16),
                        pltpu.VMEM((rows * M_HEADS, M_HEAD_DIM, M_HEAD_DIM), F32),
                        pltpu.VMEM((rows * M_HEADS, M_HEAD_DIM), F32),
                        pltpu.VMEM((rows * M_HEADS, LANES), F32)],
        compiler_params=pltpu.CompilerParams(dimension_semantics=("arbitrary", "arbitrary"),
                                             vmem_limit_bytes=VMEM_LIMIT),
        name="mlstm",
    )(qk.reshape(batch, seq, -1), vo.reshape(batch, seq, -1), gates.reshape(batch, seq, -1),
      conv_qk, gate_bias, norm_g_lanes)
    return out.reshape(batch * seq, M_WIDTH)


def _mix_out_kernel(hm_ref, ha_ref, x_ref, wm_ref, wa_ref, gpost_ref, gpre_ref, wq_ref,
                    h1_ref, xq_ref):
    def rows_stages(r0, r1):
        a = _dot(hm_ref[r0:r1, :], wm_ref[...]) + _dot(ha_ref[r0:r1, :], wa_ref[...])
        yield
        h1 = x_ref[r0:r1, :] + _rms(a, gpost_ref[...])
        h1_ref[r0:r1, :] = h1
        u = _rms(h1, gpre_ref[...]).astype(BF16)
        xq_ref[r0:r1, :] = (_dot(u, wq_ref[...])
                            * (X_HEAD_DIM ** -0.5 * LOG2E)).astype(xq_ref.dtype)

    tm = x_ref.shape[0]
    step = tm // ROW_GROUPS
    _run_staggered([rows_stages(r0, r0 + step) for r0 in range(0, tm, step)],
                   start_per_round=ROW_GROUPS)


def _mix_out(hm, ha, x2, w_m, w_a, g_post, g_pre, w_xq, tm):
    t = x2.shape[0]
    row = lambda i: (i, 0)
    return pl.pallas_call(
        _mix_out_kernel,
        grid=(t // tm,),
        in_specs=[pl.BlockSpec((tm, M_WIDTH), row),
                  pl.BlockSpec((tm, A_WIDTH), row),
                  pl.BlockSpec((tm, D_MODEL), row),
                  _const_spec(w_m.shape), _const_spec(w_a.shape),
                  _const_spec((1, D_MODEL)), _const_spec((1, D_MODEL)),
                  _const_spec(w_xq.shape)],
        out_specs=[pl.BlockSpec((tm, D_MODEL), row), pl.BlockSpec((tm, D_MODEL), row)],
        out_shape=[jax.ShapeDtypeStruct((t, D_MODEL), F32),
                   jax.ShapeDtypeStruct((t, D_MODEL), BF16)],
        compiler_params=pltpu.CompilerParams(dimension_semantics=("arbitrary",),
                                             vmem_limit_bytes=VMEM_LIMIT),
        name="mix_out",
    )(hm, ha, x2, w_m, w_a, g_post, g_pre, w_xq)


def _mem_kv_kernel(mem_ref, g_ref, w_ref, kv_ref):
    u = _rms(mem_ref[...], g_ref[...]).astype(BF16)
    kv_ref[...] = _dot(u, w_ref[...]).astype(kv_ref.dtype)


def _mem_kv(mem2, g, w_xkv, mem_len):
    t = mem2.shape[0]
    return pl.pallas_call(
        _mem_kv_kernel,
        grid=(t // mem_len,),
        in_specs=[pl.BlockSpec((mem_len, D_MODEL), lambda i: (i, 0)),
                  _const_spec((1, D_MODEL)),
                  _const_spec(w_xkv.shape)],
        out_specs=pl.BlockSpec((mem_len, 2 * D_MODEL), lambda i: (i, 0)),
        out_shape=jax.ShapeDtypeStruct((t, 2 * D_MODEL), BF16),
        compiler_params=pltpu.CompilerParams(dimension_semantics=("arbitrary",),
                                             vmem_limit_bytes=VMEM_LIMIT),
        name="mem_kv",
    )(mem2, g, w_xkv)


def _xattn_ffn_kernel(h1_ref, xq_ref, kv_ref, wo_ref, gxpost_ref, gfpre_ref, wgu_ref, wd_ref,
                      gfpost_ref, out_ref, o_ref, *, ff_chunk):
    d_ff = wd_ref.shape[0]
    chunks = [(j0, min(j0 + ff_chunk, d_ff)) for j0 in range(0, d_ff, ff_chunk)]

    def rows_stages(r0, r1):
        heads = [h * X_HEAD_DIM for h in range(X_HEADS)]
        scores = [_dot_nt(xq_ref[r0:r1, c0:c0 + X_HEAD_DIM], kv_ref[:, c0:c0 + X_HEAD_DIM])
                  for c0 in heads]
        yield
        for c0, s in zip(heads, scores):
            e = jnp.exp2(s - jnp.max(s, axis=1, keepdims=True))
            inv = 1.0 / jnp.sum(e, axis=1, keepdims=True)
            o = _dot(e.astype(BF16), kv_ref[:, D_MODEL + c0:D_MODEL + c0 + X_HEAD_DIM])
            o_ref[r0:r1, c0:c0 + X_HEAD_DIM] = (o * inv).astype(BF16)
        yield
        c = _dot(o_ref[r0:r1, :], wo_ref[...])
        yield
        h2 = h1_ref[r0:r1, :] + _rms(c, gxpost_ref[...])
        u = _rms(h2, gfpre_ref[...]).astype(BF16)
        f = None
        for j0, j1 in chunks:
            gate = _dot(u, wgu_ref[:, j0:j1])
            up = _dot(u, wgu_ref[:, d_ff + j0:d_ff + j1])
            yield
            act = (gate * jax.nn.sigmoid(gate) * up).astype(BF16)
            part = _dot(act, wd_ref[j0:j1, :])
            f = part if f is None else f + part
        yield
        out_ref[r0:r1, :] = h2 + _rms(f, gfpost_ref[...])

    tm = h1_ref.shape[0]
    step = tm // ROW_GROUPS
    _run_staggered([rows_stages(r0, r0 + step) for r0 in range(0, tm, step)],
                   start_per_round=ROW_GROUPS)


def _xattn_ffn(h1, xq, kv, w_xo, g_xpost, g_fpre, w_gu, w_d, g_fpost, tm, seq, mem_len, ff_chunk):
    t = h1.shape[0]
    row = lambda i: (i, 0)
    per_b = seq // tm
    return pl.pallas_call(
        functools.partial(_xattn_ffn_kernel, ff_chunk=ff_chunk),
        grid=(t // tm,),
        in_specs=[pl.BlockSpec((tm, D_MODEL), row),
                  pl.BlockSpec((tm, D_MODEL), row),
                  pl.BlockSpec((mem_len, 2 * D_MODEL), lambda i: (i // per_b, 0)),
                  _const_spec(w_xo.shape),
                  _const_spec((1, D_MODEL)), _const_spec((1, D_MODEL)),
                  _const_spec(w_gu.shape), _const_spec(w_d.shape),
                  _const_spec((1, D_MODEL))],
        out_specs=pl.BlockSpec((tm, D_MODEL), row),
        out_shape=jax.ShapeDtypeStruct((t, D_MODEL), F32),
        scratch_shapes=[pltpu.VMEM((tm, D_MODEL), BF16)],
        compiler_params=pltpu.CompilerParams(dimension_semantics=("arbitrary",),
                                             vmem_limit_bytes=VMEM_LIMIT),
        name="xattn_ffn",
    )(h1, xq, kv, w_xo, g_xpost, g_fpre, w_gu, w_d, g_fpost)


def _layer(h, mem2, pos2, inv_freq, p, batch, seq, mem_len):
    w_in = p["w_in"]
    o_i = 4 * M_WIDTH
    o_qa = o_i + 2 * M_HEADS
    o_ka = o_qa + A_WIDTH
    w_gate = jnp.pad(w_in[:, o_i:o_qa], ((0, 0), (0, LANES - 2 * M_HEADS)))

    def pair_interleave(w):
        w5 = w.reshape(w.shape[0], -1, 2, 2, A_HEAD_DIM // 2)
        return jnp.swapaxes(w5, 2, 3).reshape(w.shape)

    w_cat = jnp.concatenate([w_in[:, :o_i], pair_interleave(w_in[:, o_qa:o_ka]),
                             pair_interleave(w_in[:, o_ka:o_ka + A_KV_WIDTH]),
                             w_in[:, o_ka + A_KV_WIDTH:], w_gate], axis=1).astype(BF16)
    gate_bias = jnp.pad(jnp.concatenate([p["i_bias"], p["f_bias"]]),
                        (0, LANES - 2 * M_HEADS)).reshape(1, LANES)
    row = lambda v: v.reshape(1, -1)

    qk, vo, ha, gates = _in_proj(p["attn_sinks"], h, row(p["mix_pre_g"]), pos2, inv_freq, w_cat,
                                 tm=512, seq=seq)
    norm_g_lanes = jnp.broadcast_to(p["mlstm_norm_g"][:, None], (M_WIDTH, LANES))
    hm = _mlstm(qk, vo, gates, p["conv_qk"], gate_bias, norm_g_lanes, batch, seq)
    w_out = p["w_out"].astype(BF16)
    h1, xq = _mix_out(hm, ha, h, w_out[:M_WIDTH], w_out[M_WIDTH:], row(p["mix_post_g"]),
                      row(p["xattn_pre_g"]), p["w_xq"].astype(BF16), tm=1024)
    kv = _mem_kv(mem2, row(p["mem_norm_g"]), p["w_xkv"].astype(BF16), mem_len)
    return _xattn_ffn(h1, xq, kv, p["w_xo"].astype(BF16), row(p["xattn_post_g"]),
                      row(p["ffn_pre_g"]), p["w_gate_up"].astype(BF16), p["w_down"].astype(BF16),
                      row(p["ffn_post_g"]), tm=512, seq=seq, mem_len=mem_len, ff_chunk=1536)


def kernel(x, mem, positions, mix_pre_g, mix_post_g, w_in, conv_qk, f_bias, i_bias, mlstm_norm_g,
           attn_sinks, w_out, xattn_pre_g, xattn_post_g, mem_norm_g, w_xq, w_xkv, w_xo,
           ffn_pre_g, ffn_post_g, w_gate_up, w_down):
    batch, seq, _ = x.shape
    mem_len = mem.shape[1]
    depth = w_in.shape[0]
    params = dict(mix_pre_g=mix_pre_g, mix_post_g=mix_post_g, w_in=w_in, conv_qk=conv_qk,
                  f_bias=f_bias, i_bias=i_bias, mlstm_norm_g=mlstm_norm_g, attn_sinks=attn_sinks,
                  w_out=w_out, xattn_pre_g=xattn_pre_g, xattn_post_g=xattn_post_g,
                  mem_norm_g=mem_norm_g, w_xq=w_xq, w_xkv=w_xkv, w_xo=w_xo, ffn_pre_g=ffn_pre_g,
                  ffn_post_g=ffn_post_g, w_gate_up=w_gate_up, w_down=w_down)
    inv = ROPE_THETA ** (-jnp.arange(0, A_HEAD_DIM, 2, dtype=F32) / A_HEAD_DIM)
    inv_freq = jnp.tile(inv, LANES // inv.shape[0]).reshape(1, LANES)
    h = x.reshape(batch * seq, D_MODEL)
    mem2 = mem.reshape(batch * mem_len, D_MODEL)
    pos2 = positions.reshape(batch * seq, 1)
    for l in range(depth):
        h = _layer(h, mem2, pos2, inv_freq, {k: v[l] for k, v in params.items()},
                   batch, seq, mem_len)
    return h.reshape(batch, seq, D_MODEL)
```

```python
import functools

import jax
import jax.numpy as jnp
from jax import lax
from jax.experimental import pallas as pl
from jax.experimental.pallas import tpu as pltpu

D_MODEL = 1024
EPS = 1e-6
M_HEADS = 4
M_HEAD_DIM = 256
M_WIDTH = M_HEADS * M_HEAD_DIM
M_CONV = 4
M_CHUNK = 128
A_Q_HEADS = 16
A_KV_HEADS = 2
A_HEAD_DIM = 64
A_WIDTH = A_Q_HEADS * A_HEAD_DIM
A_KV_WIDTH = A_KV_HEADS * A_HEAD_DIM
WINDOW = 128
ROPE_THETA = 10000.0
LOG2E = 1.4426950408889634
X_HEADS = 4
X_HEAD_DIM = D_MODEL // X_HEADS
LANES = 128
CONV_HALO = 16
ROW_GROUPS = 2
SCORE_LOOKAHEAD = 3
EARLY_STREAM_CHUNKS = 4
VMEM_LIMIT = 56 * 1024 * 1024

F32 = jnp.float32
BF16 = jnp.bfloat16


def _rms(x, g):
    return x * lax.rsqrt(jnp.mean(x * x, axis=-1, keepdims=True) + EPS) * g


def _dot(a, b):
    return jnp.dot(a, b, preferred_element_type=F32)


def _dot_nt(a, b):
    return lax.dot_general(a, b, (((1,), (1,)), ((), ())), preferred_element_type=F32)


def _dot_tn(a, b):
    return lax.dot_general(a, b, (((0,), (0,)), ((), ())), preferred_element_type=F32)


def _run_staggered(chains, start_per_round=1):
    pending = list(chains)
    running = []
    while pending or running:
        for _ in range(min(start_per_round, len(pending))):
            running.append(pending.pop(0))
        for gen in list(running):
            if next(gen, "done") == "done":
                running.remove(gen)


def _const_spec(shape):
    nd = len(shape)
    return pl.BlockSpec(shape, lambda *_: (0,) * nd, pipeline_mode=pl.Buffered(1))


def _in_proj_kernel(sink_ref, x_ref, g_ref, pos_ref, inv_ref, w_ref, qk_ref, vo_ref, ha_ref,
                    gate_ref, q_scr, k_ref, vt_ref, *, tiles_per_seq):
    W = WINDOW
    tm = x_ref.shape[0]
    n_blk = tm // W
    half = A_HEAD_DIM // 2
    first_tile = pl.program_id(0) % tiles_per_seq == 0

    @pl.when(first_tile)
    def _():
        for s in range(4):
            k_ref[0, s * 2 * W:s * 2 * W + W, :] = jnp.zeros((W, LANES), BF16)
        vt_ref[0, :, 0:W] = jnp.zeros((LANES, W), BF16)

    @pl.when(jnp.logical_not(first_tile))
    def _():
        for s in range(4):
            k_ref[0, s * 2 * W:s * 2 * W + W, :] = k_ref[n_blk, s * 2 * W:s * 2 * W + W, :]
        vt_ref[0, :, 0:W] = vt_ref[n_blk, :, 0:W]

    u = _rms(x_ref[...], g_ref[...]).astype(BF16)

    stream_chunks = [(ref, c0, base + c0)
                     for ref, base in ((qk_ref, 0), (vo_ref, 2 * M_WIDTH))
                     for c0 in range(0, 2 * M_WIDTH, 2 * LANES)]

    def project_chunk():
        ref, c0, wc = stream_chunks.pop(0)
        ref[:, c0:c0 + 2 * LANES] = _dot(u, w_ref[:, wc:wc + 2 * LANES]).astype(ref.dtype)

    for _ in range(EARLY_STREAM_CHUNKS):
        project_chunk()
    off = 4 * M_WIDTH

    n_freq = A_HEAD_DIM // 2
    n_seg = LANES // n_freq
    rows = tm // n_seg
    seg = lax.broadcasted_iota(jnp.int32, (rows, LANES), 1) // n_freq
    pos_f = pos_ref[...].astype(F32)
    pos_c = jnp.broadcast_to(pos_f[0:rows], (rows, LANES))
    for s in range(1, n_seg):
        pos_c = jnp.where(seg == s, pos_f[s * rows:(s + 1) * rows], pos_c)
    ang_c = pos_c * inv_ref[...]

    def spread(table_c):
        rolled = [table_c] + [pltpu.roll(table_c, n_freq * k, 1) for k in range(1, n_seg)]
        chunks = []
        for s in range(n_seg):
            full = rolled[(0 - s) % n_seg]
            for p in range(1, n_seg):
                full = jnp.where(seg == p, rolled[(p - s) % n_seg], full)
            chunks.append(full)
        return jnp.concatenate(chunks, axis=0)

    lane = lax.broadcasted_iota(jnp.int32, (tm, LANES), 1)
    cos_t = spread(jnp.cos(ang_c))
    sin_t = spread(jnp.sin(ang_c))
    sin_t = jnp.where(lane < LANES // 2, -sin_t, sin_t)

    def rope(t, c, s):
        return t * c + pltpu.roll(t, LANES // 2, 1) * s

    scale = A_HEAD_DIM ** -0.5 * LOG2E
    cos_q = cos_t * scale
    sin_q = sin_t * scale
    o_kv = off + A_WIDTH
    kv = _dot(u, w_ref[:, o_kv:o_kv + 2 * LANES])
    k_rot = rope(kv[:, 0:LANES], cos_t, sin_t)
    slot_a = (lane % A_HEAD_DIM) < half
    variants = (jnp.where(slot_a, k_rot, 0.0),
                jnp.where(slot_a, 0.0, pltpu.roll(k_rot, half, 1)),
                jnp.where(slot_a, pltpu.roll(k_rot, LANES - half, 1), 0.0),
                jnp.where(slot_a, 0.0, k_rot))
    for j in range(n_blk):
        for s, var in enumerate(variants):
            blk = var[j * W:(j + 1) * W, :].astype(BF16)
            k_ref[j, s * 2 * W + W:(s + 1) * 2 * W, :] = blk
            k_ref[j + 1, s * 2 * W:s * 2 * W + W, :] = blk
        vt = kv[j * W:(j + 1) * W, LANES:2 * LANES].T.astype(BF16)
        vt_ref[j, :, W:2 * W] = vt
        vt_ref[j + 1, :, 0:W] = vt

    for j in range(A_WIDTH // (2 * LANES)):
        t = _dot(u, w_ref[:, off:off + 2 * LANES])
        for i in range(2):
            c0 = (2 * j + i) * LANES
            q_scr[:, c0:c0 + LANES] = rope(t[:, i * LANES:(i + 1) * LANES],
                                           cos_q, sin_q).astype(q_scr.dtype)
        off += 2 * LANES
    gate_ref[...] = _dot(u, w_ref[:, o_kv + 2 * LANES:o_kv + 3 * LANES])

    kpos = lax.broadcasted_iota(jnp.int32, (2 * W, W), 0)
    qpos = lax.broadcasted_iota(jnp.int32, (2 * W, W), 1)
    diff = W + qpos - kpos
    band = (diff >= 0) & (diff < W)
    bias_rest = jnp.where(band, 0.0, -jnp.inf).astype(F32)
    bias_first = jnp.where(band & (jnp.logical_not(first_tile) | (kpos >= W)),
                           0.0, -jnp.inf).astype(F32)

    n_pairs = A_Q_HEADS // 2
    pairs_per_group = n_pairs // A_KV_HEADS
    items = [(j, pair) for j in range(n_blk) for pair in range(n_pairs)]

    def scores(j, pair):
        g = pair // pairs_per_group
        q = q_scr[j * W:(j + 1) * W, pair * LANES:(pair + 1) * LANES]
        return _dot_nt(k_ref[j, g * 4 * W:(g + 1) * 4 * W, :], q)

    n_late = len(stream_chunks)
    pending = [scores(*it) for it in items[:SCORE_LOOKAHEAD]]
    for idx, (j, pair) in enumerate(items):
        g = pair // pairs_per_group
        if idx + SCORE_LOOKAHEAD < len(items):
            pending.append(scores(*items[idx + SCORE_LOOKAHEAD]))
        st_pair = pending.pop(0)
        bias = bias_first if j == 0 else bias_rest
        pes, inv_dens = [], []
        for slot in range(2):
            sink = sink_ref[2 * pair + slot] * LOG2E
            st = st_pair[slot * 2 * W:(slot + 1) * 2 * W, :] + bias
            mx = jnp.maximum(jnp.max(st, axis=0, keepdims=True), sink)
            pe = jnp.exp2(st - mx)
            den = jnp.sum(pe, axis=0, keepdims=True) + jnp.exp2(sink - mx)
            pes.append(pe.astype(BF16))
            inv_dens.append(1.0 / den)
        ot = _dot(vt_ref[j, g * A_HEAD_DIM:(g + 1) * A_HEAD_DIM, :],
                  jnp.concatenate(pes, axis=1))
        o_pair = jnp.concatenate([ot[:, slot * W:(slot + 1) * W] * inv_dens[slot]
                                  for slot in range(2)], axis=0)
        ha_ref[j * W:(j + 1) * W, pair * LANES:(pair + 1) * LANES] = (
            o_pair.T.astype(ha_ref.dtype))
        if (idx + 1) * n_late // len(items) > idx * n_late // len(items):
            project_chunk()
    assert not stream_chunks


def _in_proj(sinks, x2, g, pos, inv_freq, w_cat, tm, seq):
    t = x2.shape[0]
    widths = (2 * M_WIDTH, 2 * M_WIDTH, A_WIDTH, LANES)
    dtypes = (BF16, BF16, BF16, F32)
    n_blk = tm // WINDOW
    return pl.pallas_call(
        functools.partial(_in_proj_kernel, tiles_per_seq=seq // tm),
        grid=(t // tm,),
        in_specs=[pl.BlockSpec(memory_space=pltpu.SMEM),
                  pl.BlockSpec((tm, D_MODEL), lambda i: (i, 0)),
                  _const_spec((1, D_MODEL)),
                  pl.BlockSpec((tm, 1), lambda i: (i, 0)),
                  _const_spec((1, LANES)),
                  _const_spec(w_cat.shape)],
        out_specs=[pl.BlockSpec((tm, w), lambda i: (i, 0)) for w in widths],
        out_shape=[jax.ShapeDtypeStruct((t, w), d) for w, d in zip(widths, dtypes)],
        scratch_shapes=[pltpu.VMEM((tm, A_WIDTH), BF16),
                        pltpu.VMEM((n_blk + 1, 4 * 2 * WINDOW, LANES), BF16),
                        pltpu.VMEM((n_blk + 1, LANES, 2 * WINDOW), BF16)],
        compiler_params=pltpu.CompilerParams(dimension_semantics=("arbitrary",),
                                             vmem_limit_bytes=VMEM_LIMIT),
        name="in_proj",
    )(sinks, x2, g, pos, inv_freq, w_cat)


def _mlstm_mix_kernel(qk_ref, vo_ref, gate_ref, conv_ref, bias_ref, ngb_ref, ha_ref, x_ref,
                      wm_ref, wa_ref, gpost_ref, gpre_ref, wq_ref, h1_ref, xq_ref,
                      hist_ref, ct_ref, n_ref, m_ref, hm_ref, *, tiles_per_seq):
    L = M_CHUNK
    D = M_HEAD_DIM
    tm = qk_ref.shape[0]
    step = pl.program_id(0)
    write_slot = step % 2
    read_slot = (step + 1) % 2

    @pl.when(step % tiles_per_seq == 0)
    def _():
        hist_ref[0:CONV_HALO, :] = jnp.zeros((CONV_HALO, 2 * M_WIDTH), BF16)
        ct_ref[...] = jnp.zeros_like(ct_ref)
        n_ref[...] = jnp.zeros_like(n_ref)
        m_ref[...] = jnp.zeros_like(m_ref)

    @pl.when(step == 0)
    def _():
        hm_ref[1] = jnp.zeros(hm_ref.shape[1:], hm_ref.dtype)

    hist_ref[CONV_HALO:CONV_HALO + tm, :] = qk_ref[...]

    sel_r = lax.broadcasted_iota(jnp.int32, (L, CONV_HALO + L), 0)
    sel_c = lax.broadcasted_iota(jnp.int32, (L, CONV_HALO + L), 1)
    shift_sel = jnp.concatenate(
        [jnp.where(sel_c == sel_r + (CONV_HALO - (M_CONV - 1) + j), 1.0, 0.0)
         for j in range(M_CONV - 1)], axis=0).astype(BF16)

    row = lax.broadcasted_iota(jnp.int32, (L, L), 0)
    col = lax.broadcasted_iota(jnp.int32, (L, L), 1)
    tril = jnp.where(col <= row, 1.0, 0.0).astype(F32)
    key_le_query = row <= col
    sub8 = lax.broadcasted_iota(jnp.int32, (8, 1), 0)

    def gate_terms(c):
        gates = gate_ref[c * L:(c + 1) * L, :] + bias_ref[...]
        bcum = jnp.dot(tril, jax.nn.log_sigmoid(gates), precision=lax.Precision.HIGHEST,
                       preferred_element_type=F32)
        return gates, bcum, gates.T, bcum.T

    def hi_lo_rows(v):
        hi = v.astype(BF16).astype(F32)
        return jnp.where(sub8 == 0, hi, jnp.where(sub8 == 1, v - hi, 0.0)).astype(BF16)

    def conv_silu(c, shifted, c0, scale):
        r0 = CONV_HALO + c * L
        acc = (hist_ref[r0:r0 + L, c0:c0 + D].astype(F32)
               * conv_ref[M_CONV - 1:M_CONV, c0:c0 + D])
        for j in range(M_CONV - 1):
            acc = acc + shifted[j * L:(j + 1) * L, :] * conv_ref[j:j + 1, c0:c0 + D]
        half = acc.astype(BF16) * (0.5 * scale)
        arg = half if scale == 1.0 else half * (1.0 / scale)
        return half * jnp.tanh(arg) + half

    def shifted_rows(c, h):
        w0 = c * L
        return [_dot(shift_sel, hist_ref[w0:w0 + CONV_HALO + L, cc:cc + D])
                for cc in (h * D, M_WIDTH + h * D)]

    def conv_qk(c, h, shifted):
        qb = conv_silu(c, shifted[0], h * D, 1.0)
        kb = conv_silu(c, shifted[1], M_WIDTH + h * D, M_HEAD_DIM ** -0.5)
        return qb, kb, _dot_nt(kb, qb)

    def head_chain(h, all_terms):
        n_chunks = len(all_terms)
        shifted = shifted_rows(0, h)
        yield
        nxt = conv_qk(0, h, shifted)
        yield
        for c, terms in enumerate(all_terms):
            qb, kb, st = nxt
            inter_t = _dot_nt(ct_ref[h].astype(BF16), qb)
            qn2 = _dot_nt(hi_lo_rows(n_ref[h:h + 1, :]), qb)
            if c + 1 < n_chunks:
                shifted = shifted_rows(c + 1, h)
            yield
            finish = chunk_body(c, h, terms, qb, kb, st, inter_t, qn2)
            next(finish)
            yield
            if c + 1 < n_chunks:
                nxt = conv_qk(c + 1, h, shifted)
            next(finish, None)
            yield

    def chunk_body(c, h, terms, qb, kb, st, inter_t, qn2):
        gates, bcum, gates_t, bcum_t = terms
        c0 = h * D
        sr = h
        b_row = bcum_t[M_HEADS + h:M_HEADS + h + 1, :]
        i_row = gates_t[h:h + 1, :]
        c_col = gates[:, h:h + 1] - bcum[:, M_HEADS + h:M_HEADS + h + 1]
        m_prev = m_ref[sr:sr + 1, 0:1]
        g_row = b_row + m_prev
        dm_t = jnp.where(key_le_query, c_col + b_row, -jnp.inf)
        mj = jnp.maximum(g_row, jnp.max(dm_t, axis=0, keepdims=True))
        s_t = st * jnp.exp(dm_t - mj)
        vt = vo_ref[c * L:(c + 1) * L, c0:c0 + D].astype(F32).T
        intra_t = _dot(vt.astype(BF16), s_t.astype(BF16))
        b_last = b_row[:, L - 1:L]
        w_row = b_last - b_row + i_row
        m_new = jnp.maximum(b_last + m_prev, jnp.max(w_row, axis=1, keepdims=True))
        decay = jnp.exp(b_last + m_prev - m_new)
        wi_row = jnp.exp(w_row - m_new)
        ct_new = _dot((vt * wi_row).astype(BF16), kb)
        n_new2 = _dot(hi_lo_rows(wi_row), kb)
        yield
        wg = jnp.exp(g_row - mj)
        num_t = wg * inter_t + intra_t
        den = wg * (qn2[0:1, :] + qn2[1:2, :]) + jnp.sum(s_t, axis=0, keepdims=True)
        hh_t = num_t * (1.0 / jnp.maximum(jnp.abs(den), jnp.exp(-mj)))
        rs = lax.rsqrt(jnp.mean(hh_t * hh_t, axis=0, keepdims=True) + EPS)
        hn_t = hh_t * rs * ngb_ref[c0:c0 + D, :]
        og = vo_ref[c * L:(c + 1) * L, M_WIDTH + c0:M_WIDTH + c0 + D]
        sig = jnp.tanh(og * 0.5) * 0.5 + 0.5
        hm_ref[write_slot, c * L:(c + 1) * L, c0:c0 + D] = sig * hn_t.T.astype(BF16)
        ct_ref[sr] = decay * ct_ref[sr] + ct_new
        n_ref[sr:sr + 1, :] = decay * n_ref[sr:sr + 1, :] + n_new2[0:1, :] + n_new2[1:2, :]
        m_ref[sr:sr + 1, :] = jnp.broadcast_to(m_new, (1, LANES))

    def mix_chain():
        n_col = 2 * LANES
        parts = []
        for c0 in range(0, D_MODEL, n_col):
            part = _dot(hm_ref[read_slot], wm_ref[:, c0:c0 + n_col])
            yield
            parts.append(part + _dot(ha_ref[...], wa_ref[:, c0:c0 + n_col]))
            yield
        h1 = x_ref[...] + _rms(jnp.concatenate(parts, axis=1), gpost_ref[...])
        h1_ref[...] = h1
        u = _rms(h1, gpre_ref[...]).astype(BF16)
        yield
        for c0 in range(0, D_MODEL, n_col):
            xq_ref[:, c0:c0 + n_col] = (_dot(u, wq_ref[:, c0:c0 + n_col])
                                        * (X_HEAD_DIM ** -0.5 * LOG2E)).astype(xq_ref.dtype)
            yield

    all_terms = [gate_terms(c) for c in range(tm // L)]
    chains = [head_chain(h, all_terms) for h in range(M_HEADS)] + [mix_chain()]
    _run_staggered(chains, start_per_round=len(chains))

    hist_ref[0:CONV_HALO, :] = hist_ref[tm:tm + CONV_HALO, :]


def _mlstm_mix(qk, vo, gates, conv_qk, gate_bias, norm_g_lanes, ha, x2, w_m, w_a, g_post, g_pre,
               w_xq, tm, seq):
    t = x2.shape[0]
    n_tiles = t // tm
    cur = lambda s: (jnp.minimum(s, n_tiles - 1), 0)
    prev = lambda s: (jnp.maximum(s - 1, 0), 0)
    return pl.pallas_call(
        functools.partial(_mlstm_mix_kernel, tiles_per_seq=seq // tm),
        grid=(n_tiles + 1,),
        in_specs=[pl.BlockSpec((tm, 2 * M_WIDTH), cur),
                  pl.BlockSpec((tm, 2 * M_WIDTH), cur),
                  pl.BlockSpec((tm, LANES), cur),
                  _const_spec((M_CONV, 2 * M_WIDTH)),
                  _const_spec((1, LANES)),
                  _const_spec((M_WIDTH, LANES)),
                  pl.BlockSpec((tm, A_WIDTH), prev),
                  pl.BlockSpec((tm, D_MODEL), prev),
                  _const_spec(w_m.shape), _const_spec(w_a.shape),
                  _const_spec((1, D_MODEL)), _const_spec((1, D_MODEL)),
                  _const_spec(w_xq.shape)],
        out_specs=[pl.BlockSpec((tm, D_MODEL), prev), pl.BlockSpec((tm, D_MODEL), prev)],
        out_shape=[jax.ShapeDtypeStruct((t, D_MODEL), F32),
                   jax.ShapeDtypeStruct((t, D_MODEL), BF16)],
        scratch_shapes=[pltpu.VMEM((tm + CONV_HALO, 2 * M_WIDTH), BF16),
                        pltpu.VMEM((M_HEADS, M_HEAD_DIM, M_HEAD_DIM), F32),
                        pltpu.VMEM((M_HEADS, M_HEAD_DIM), F32),
                        pltpu.VMEM((M_HEADS, LANES), F32),
                        pltpu.VMEM((2, tm, M_WIDTH), BF16)],
        compiler_params=pltpu.CompilerParams(dimension_semantics=("arbitrary",),
                                             vmem_limit_bytes=VMEM_LIMIT),
        name="mlstm_mix",
    )(qk, vo, gates, conv_qk, gate_bias, norm_g_lanes, ha, x2, w_m, w_a, g_post, g_pre, w_xq)


def _mem_kv_kernel(mem_ref, g_ref, w_ref, kv_ref):
    u = _rms(mem_ref[...], g_ref[...]).astype(BF16)
    kv_ref[...] = _dot(u, w_ref[...]).astype(kv_ref.dtype)


def _mem_kv(mem2, g, w_xkv, mem_len):
    t = mem2.shape[0]
    return pl.pallas_call(
        _mem_kv_kernel,
        grid=(t // mem_len,),
        in_specs=[pl.BlockSpec((mem_len, D_MODEL), lambda i: (i, 0)),
                  _const_spec((1, D_MODEL)),
                  _const_spec(w_xkv.shape)],
        out_specs=pl.BlockSpec((mem_len, 2 * D_MODEL), lambda i: (i, 0)),
        out_shape=jax.ShapeDtypeStruct((t, 2 * D_MODEL), BF16),
        compiler_params=pltpu.CompilerParams(dimension_semantics=("arbitrary",),
                                             vmem_limit_bytes=VMEM_LIMIT),
        name="mem_kv",
    )(mem2, g, w_xkv)


def _xattn_ffn_kernel(h1_ref, xq_ref, kv_ref, wo_ref, gxpost_ref, gfpre_ref, wgu_ref, wd_ref,
                      gfpost_ref, out_ref, o_ref, *, ff_chunk):
    d_ff = wd_ref.shape[0]
    chunks = [(j0, min(j0 + ff_chunk, d_ff)) for j0 in range(0, d_ff, ff_chunk)]

    def rows_stages(r0, r1):
        heads = [h * X_HEAD_DIM for h in range(X_HEADS)]
        scores = [_dot_nt(xq_ref[r0:r1, c0:c0 + X_HEAD_DIM], kv_ref[:, c0:c0 + X_HEAD_DIM])
                  for c0 in heads]
        yield
        for c0, s in zip(heads, scores):
            e = jnp.exp2(s - jnp.max(s, axis=1, keepdims=True))
            inv = 1.0 / jnp.sum(e, axis=1, keepdims=True)
            o = _dot(e.astype(BF16), kv_ref[:, D_MODEL + c0:D_MODEL + c0 + X_HEAD_DIM])
            o_ref[r0:r1, c0:c0 + X_HEAD_DIM] = (o * inv).astype(BF16)
        yield
        c = _dot(o_ref[r0:r1, :], wo_ref[...])
        yield
        h2 = h1_ref[r0:r1, :] + _rms(c, gxpost_ref[...])
        u = _rms(h2, gfpre_ref[...]).astype(BF16)
        f = None
        for j0, j1 in chunks:
            gate = _dot(u, wgu_ref[:, j0:j1])
            up = _dot(u, wgu_ref[:, d_ff + j0:d_ff + j1])
            yield
            act = (gate * jax.nn.sigmoid(gate) * up).astype(BF16)
            part = _dot(act, wd_ref[j0:j1, :])
            f = part if f is None else f + part
        yield
        out_ref[r0:r1, :] = h2 + _rms(f, gfpost_ref[...])

    tm = h1_ref.shape[0]
    step = tm // ROW_GROUPS
    _run_staggered([rows_stages(r0, r0 + step) for r0 in range(0, tm, step)],
                   start_per_round=ROW_GROUPS)


def _xattn_ffn(h1, xq, kv, w_xo, g_xpost, g_fpre, w_gu, w_d, g_fpost, tm, seq, mem_len, ff_chunk):
    t = h1.shape[0]
    row = lambda i: (i, 0)
    per_b = seq // tm
    return pl.pallas_call(
        functools.partial(_xattn_ffn_kernel, ff_chunk=ff_chunk),
        grid=(t // tm,),
        in_specs=[pl.BlockSpec((tm, D_MODEL), row),
                  pl.BlockSpec((tm, D_MODEL), row),
                  pl.BlockSpec((mem_len, 2 * D_MODEL), lambda i: (i // per_b, 0)),
                  _const_spec(w_xo.shape),
                  _const_spec((1, D_MODEL)), _const_spec((1, D_MODEL)),
                  _const_spec(w_gu.shape), _const_spec(w_d.shape),
                  _const_spec((1, D_MODEL))],
        out_specs=pl.BlockSpec((tm, D_MODEL), row),
        out_shape=jax.ShapeDtypeStruct((t, D_MODEL), F32),
        scratch_shapes=[pltpu.VMEM((tm, D_MODEL), BF16)],
        compiler_params=pltpu.CompilerParams(dimension_semantics=("arbitrary",),
                                             vmem_limit_bytes=VMEM_LIMIT),
        name="xattn_ffn",
    )(h1, xq, kv, w_xo, g_xpost, g_fpre, w_gu, w_d, g_fpost)


def _layer(h, mem2, pos2, inv_freq, p, batch, seq, mem_len):
    w_in = p["w_in"]
    o_i = 4 * M_WIDTH
    o_qa = o_i + 2 * M_HEADS
    o_ka = o_qa + A_WIDTH
    w_gate = jnp.pad(w_in[:, o_i:o_qa], ((0, 0), (0, LANES - 2 * M_HEADS)))

    def pair_interleave(w):
        w5 = w.reshape(w.shape[0], -1, 2, 2, A_HEAD_DIM // 2)
        return jnp.swapaxes(w5, 2, 3).reshape(w.shape)

    w_cat = jnp.concatenate([w_in[:, :o_i], pair_interleave(w_in[:, o_qa:o_ka]),
                             pair_interleave(w_in[:, o_ka:o_ka + A_KV_WIDTH]),
                             w_in[:, o_ka + A_KV_WIDTH:], w_gate], axis=1).astype(BF16)
    gate_bias = jnp.pad(jnp.concatenate([p["i_bias"], p["f_bias"]]),
                        (0, LANES - 2 * M_HEADS)).reshape(1, LANES)
    row = lambda v: v.reshape(1, -1)

    qk, vo, ha, gates = _in_proj(p["attn_sinks"], h, row(p["mix_pre_g"]), pos2, inv_freq, w_cat,
                                 tm=512, seq=seq)
    norm_g_lanes = jnp.broadcast_to(p["mlstm_norm_g"][:, None], (M_WIDTH, LANES))
    w_out = p["w_out"].astype(BF16)
    h1, xq = _mlstm_mix(qk, vo, gates, p["conv_qk"], gate_bias, norm_g_lanes, ha, h,
                        w_out[:M_WIDTH], w_out[M_WIDTH:], row(p["mix_post_g"]),
                        row(p["xattn_pre_g"]), p["w_xq"].astype(BF16), tm=512, seq=seq)
    kv = _mem_kv(mem2, row(p["mem_norm_g"]), p["w_xkv"].astype(BF16), mem_len)
    return _xattn_ffn(h1, xq, kv, p["w_xo"].astype(BF16), row(p["xattn_post_g"]),
                      row(p["ffn_pre_g"]), p["w_gate_up"].astype(BF16), p["w_down"].astype(BF16),
                      row(p["ffn_post_g"]), tm=512, seq=seq, mem_len=mem_len, ff_chunk=1536)


def kernel(x, mem, positions, mix_pre_g, mix_post_g, w_in, conv_qk, f_bias, i_bias, mlstm_norm_g,
           attn_sinks, w_out, xattn_pre_g, xattn_post_g, mem_norm_g, w_xq, w_xkv, w_xo,
           ffn_pre_g, ffn_post_g, w_gate_up, w_down):
    batch, seq, _ = x.shape
    mem_len = mem.shape[1]
    depth = w_in.shape[0]
    params = dict(mix_pre_g=mix_pre_g, mix_post_g=mix_post_g, w_in=w_in, conv_qk=conv_qk,
                  f_bias=f_bias, i_bias=i_bias, mlstm_norm_g=mlstm_norm_g, attn_sinks=attn_sinks,
                  w_out=w_out, xattn_pre_g=xattn_pre_g, xattn_post_g=xattn_post_g,
                  mem_norm_g=mem_norm_g, w_xq=w_xq, w_xkv=w_xkv, w_xo=w_xo, ffn_pre_g=ffn_pre_g,
                  ffn_post_g=ffn_post_g, w_gate_up=w_gate_up, w_down=w_down)
    inv = ROPE_THETA ** (-jnp.arange(0, A_HEAD_DIM, 2, dtype=F32) / A_HEAD_DIM)
    inv_freq = jnp.tile(inv, LANES // inv.shape[0]).reshape(1, LANES)
    h = x.reshape(batch * seq, D_MODEL)
    mem2 = mem.reshape(batch * mem_len, D_MODEL)
    pos2 = positions.reshape(batch * seq, 1)
    for l in range(depth):
        h = _layer(h, mem2, pos2, inv_freq, {k: v[l] for k, v in params.items()},
                   batch, seq, mem_len)
    return h.reshape(batch, seq, D_MODEL)
```

```python
import functools

import jax
import jax.numpy as jnp
from jax import lax
from jax.experimental import pallas as pl
from jax.experimental.pallas import tpu as pltpu

D_MODEL = 1024
EPS = 1e-6
M_HEADS = 4
M_HEAD_DIM = 256
M_WIDTH = M_HEADS * M_HEAD_DIM
M_CONV = 4
M_CHUNK = 128
A_Q_HEADS = 16
A_KV_HEADS = 2
A_HEAD_DIM = 64
A_WIDTH = A_Q_HEADS * A_HEAD_DIM
A_KV_WIDTH = A_KV_HEADS * A_HEAD_DIM
WINDOW = 128
ROPE_THETA = 10000.0
LOG2E = 1.4426950408889634
X_HEADS = 4
X_HEAD_DIM = D_MODEL // X_HEADS
LANES = 128
CONV_HALO = 16
ROW_GROUPS = 2
SCORE_LOOKAHEAD = 3
EARLY_STREAM_CHUNKS = 2
VMEM_LIMIT = 56 * 1024 * 1024

F32 = jnp.float32
BF16 = jnp.bfloat16


def _rms(x, g):
    return x * lax.rsqrt(jnp.mean(x * x, axis=-1, keepdims=True) + EPS) * g


def _dot(a, b):
    return jnp.dot(a, b, preferred_element_type=F32)


def _dot_nt(a, b):
    return lax.dot_general(a, b, (((1,), (1,)), ((), ())), preferred_element_type=F32)


def _dot_tn(a, b):
    return lax.dot_general(a, b, (((0,), (0,)), ((), ())), preferred_element_type=F32)


def _run_staggered(chains, start_per_round=1):
    pending = list(chains)
    running = []
    while pending or running:
        for _ in range(min(start_per_round, len(pending))):
            running.append(pending.pop(0))
        for gen in list(running):
            if next(gen, "done") == "done":
                running.remove(gen)


def _const_spec(shape):
    nd = len(shape)
    return pl.BlockSpec(shape, lambda *_: (0,) * nd, pipeline_mode=pl.Buffered(1))


def _in_proj_kernel(sink_ref, x_ref, g_ref, pos_ref, inv_ref, w_ref, qk_ref, vo_ref, ha_ref,
                    gate_ref, q_scr, k_ref, vt_ref, *, tiles_per_seq):
    W = WINDOW
    tm = x_ref.shape[0]
    n_blk = tm // W
    half = A_HEAD_DIM // 2
    first_tile = pl.program_id(0) % tiles_per_seq == 0

    @pl.when(first_tile)
    def _():
        for s in range(4):
            k_ref[0, s * 2 * W:s * 2 * W + W, :] = jnp.zeros((W, LANES), BF16)
        vt_ref[0, :, 0:W] = jnp.zeros((LANES, W), BF16)

    @pl.when(jnp.logical_not(first_tile))
    def _():
        for s in range(4):
            k_ref[0, s * 2 * W:s * 2 * W + W, :] = k_ref[n_blk, s * 2 * W:s * 2 * W + W, :]
        vt_ref[0, :, 0:W] = vt_ref[n_blk, :, 0:W]

    u = _rms(x_ref[...], g_ref[...]).astype(BF16)

    stream_chunks = [(ref, c0, base + c0)
                     for ref, base in ((qk_ref, 0), (vo_ref, 2 * M_WIDTH))
                     for c0 in range(0, 2 * M_WIDTH, 2 * LANES)]

    def project_chunk():
        ref, c0, wc = stream_chunks.pop(0)
        ref[:, c0:c0 + 2 * LANES] = _dot(u, w_ref[:, wc:wc + 2 * LANES]).astype(ref.dtype)

    for _ in range(EARLY_STREAM_CHUNKS):
        project_chunk()
    off = 4 * M_WIDTH

    n_freq = A_HEAD_DIM // 2
    n_seg = LANES // n_freq
    rows = tm // n_seg
    seg = lax.broadcasted_iota(jnp.int32, (rows, LANES), 1) // n_freq
    pos_f = pos_ref[...].astype(F32)
    pos_c = jnp.broadcast_to(pos_f[0:rows], (rows, LANES))
    for s in range(1, n_seg):
        pos_c = jnp.where(seg == s, pos_f[s * rows:(s + 1) * rows], pos_c)
    ang_c = pos_c * inv_ref[...]

    def spread(table_c):
        rolled = [table_c] + [pltpu.roll(table_c, n_freq * k, 1) for k in range(1, n_seg)]
        chunks = []
        for s in range(n_seg):
            full = rolled[(0 - s) % n_seg]
            for p in range(1, n_seg):
                full = jnp.where(seg == p, rolled[(p - s) % n_seg], full)
            chunks.append(full)
        return jnp.concatenate(chunks, axis=0)

    lane = lax.broadcasted_iota(jnp.int32, (tm, LANES), 1)
    cos_t = spread(jnp.cos(ang_c))
    sin_t = spread(jnp.sin(ang_c))
    sin_t = jnp.where(lane < LANES // 2, -sin_t, sin_t)

    def rope(t, c, s):
        return t * c + pltpu.roll(t, LANES // 2, 1) * s

    scale = A_HEAD_DIM ** -0.5 * LOG2E
    cos_q = cos_t * scale
    sin_q = sin_t * scale
    o_kv = off + A_WIDTH
    kv = _dot(u, w_ref[:, o_kv:o_kv + 2 * LANES])
    k_rot = rope(kv[:, 0:LANES], cos_t, sin_t)
    slot_a = (lane % A_HEAD_DIM) < half
    variants = (jnp.where(slot_a, k_rot, 0.0),
                jnp.where(slot_a, 0.0, pltpu.roll(k_rot, half, 1)),
                jnp.where(slot_a, pltpu.roll(k_rot, LANES - half, 1), 0.0),
                jnp.where(slot_a, 0.0, k_rot))
    for j in range(n_blk):
        for s, var in enumerate(variants):
            blk = var[j * W:(j + 1) * W, :].astype(BF16)
            k_ref[j, s * 2 * W + W:(s + 1) * 2 * W, :] = blk
            k_ref[j + 1, s * 2 * W:s * 2 * W + W, :] = blk
        vt = kv[j * W:(j + 1) * W, LANES:2 * LANES].T.astype(BF16)
        vt_ref[j, :, W:2 * W] = vt
        vt_ref[j + 1, :, 0:W] = vt

    for j in range(A_WIDTH // (2 * LANES)):
        t = _dot(u, w_ref[:, off:off + 2 * LANES])
        for i in range(2):
            c0 = (2 * j + i) * LANES
            q_scr[:, c0:c0 + LANES] = rope(t[:, i * LANES:(i + 1) * LANES],
                                           cos_q, sin_q).astype(q_scr.dtype)
        off += 2 * LANES
    gate_ref[...] = _dot(u, w_ref[:, o_kv + 2 * LANES:o_kv + 3 * LANES])

    kpos = lax.broadcasted_iota(jnp.int32, (2 * W, W), 0)
    qpos = lax.broadcasted_iota(jnp.int32, (2 * W, W), 1)
    diff = W + qpos - kpos
    band = (diff >= 0) & (diff < W)
    bias_rest = jnp.where(band, 0.0, -jnp.inf).astype(F32)
    bias_first = jnp.where(band & (jnp.logical_not(first_tile) | (kpos >= W)),
                           0.0, -jnp.inf).astype(F32)

    n_pairs = A_Q_HEADS // 2
    pairs_per_group = n_pairs // A_KV_HEADS
    items = [(j, pair) for j in range(n_blk) for pair in range(n_pairs)]

    def scores(j, pair):
        g = pair // pairs_per_group
        q = q_scr[j * W:(j + 1) * W, pair * LANES:(pair + 1) * LANES]
        return _dot_nt(k_ref[j, g * 4 * W:(g + 1) * 4 * W, :], q)

    n_late = len(stream_chunks)
    pending = [scores(*it) for it in items[:SCORE_LOOKAHEAD]]
    for idx, (j, pair) in enumerate(items):
        g = pair // pairs_per_group
        if idx + SCORE_LOOKAHEAD < len(items):
            pending.append(scores(*items[idx + SCORE_LOOKAHEAD]))
        st_pair = pending.pop(0)
        bias = bias_first if j == 0 else bias_rest
        pes, inv_dens = [], []
        for slot in range(2):
            sink = sink_ref[2 * pair + slot] * LOG2E
            st = st_pair[slot * 2 * W:(slot + 1) * 2 * W, :] + bias
            mx = jnp.maximum(jnp.max(st, axis=0, keepdims=True), sink)
            pe = jnp.exp2(st - mx)
            den = jnp.sum(pe, axis=0, keepdims=True) + jnp.exp2(sink - mx)
            pes.append(pe.astype(BF16))
            inv_dens.append(1.0 / den)
        ot = _dot(vt_ref[j, g * A_HEAD_DIM:(g + 1) * A_HEAD_DIM, :],
                  jnp.concatenate(pes, axis=1))
        o_pair = jnp.concatenate([ot[:, slot * W:(slot + 1) * W] * inv_dens[slot]
                                  for slot in range(2)], axis=0)
        ha_ref[j * W:(j + 1) * W, pair * LANES:(pair + 1) * LANES] = (
            o_pair.T.astype(ha_ref.dtype))
        if (idx + 1) * n_late // len(items) > idx * n_late // len(items):
            project_chunk()
    assert not stream_chunks


def _in_proj(sinks, x2, g, pos, inv_freq, w_cat, tm, seq):
    t = x2.shape[0]
    widths = (2 * M_WIDTH, 2 * M_WIDTH, A_WIDTH, LANES)
    dtypes = (BF16, BF16, BF16, F32)
    n_blk = tm // WINDOW
    return pl.pallas_call(
        functools.partial(_in_proj_kernel, tiles_per_seq=seq // tm),
        grid=(t // tm,),
        in_specs=[pl.BlockSpec(memory_space=pltpu.SMEM),
                  pl.BlockSpec((tm, D_MODEL), lambda i: (i, 0)),
                  _const_spec((1, D_MODEL)),
                  pl.BlockSpec((tm, 1), lambda i: (i, 0)),
                  _const_spec((1, LANES)),
                  _const_spec(w_cat.shape)],
        out_specs=[pl.BlockSpec((tm, w), lambda i: (i, 0)) for w in widths],
        out_shape=[jax.ShapeDtypeStruct((t, w), d) for w, d in zip(widths, dtypes)],
        scratch_shapes=[pltpu.VMEM((tm, A_WIDTH), BF16),
                        pltpu.VMEM((n_blk + 1, 4 * 2 * WINDOW, LANES), BF16),
                        pltpu.VMEM((n_blk + 1, LANES, 2 * WINDOW), BF16)],
        compiler_params=pltpu.CompilerParams(dimension_semantics=("arbitrary",),
                                             vmem_limit_bytes=VMEM_LIMIT),
        name="in_proj",
    )(sinks, x2, g, pos, inv_freq, w_cat)


def _mlstm_mix_kernel(qk_ref, vo_ref, gate_ref, conv_ref, bias_ref, ngb_ref, ha_ref, x_ref,
                      wo_ref, gpost_ref, gpre_ref, wq_ref, h1_ref, xq_ref,
                      hist_ref, ct_ref, n_ref, m_ref, hm_ref, *, tiles_per_seq):
    L = M_CHUNK
    D = M_HEAD_DIM
    tm = qk_ref.shape[0]
    step = pl.program_id(0)
    write_slot = step % 2
    read_slot = (step + 1) % 2

    @pl.when(step % tiles_per_seq == 0)
    def _():
        hist_ref[0:CONV_HALO, :] = jnp.zeros((CONV_HALO, 2 * M_WIDTH), BF16)
        ct_ref[...] = jnp.zeros_like(ct_ref)
        n_ref[...] = jnp.zeros_like(n_ref)
        m_ref[...] = jnp.zeros_like(m_ref)

    @pl.when(step == 0)
    def _():
        hm_ref[1] = jnp.zeros(hm_ref.shape[1:], hm_ref.dtype)

    hist_ref[CONV_HALO:CONV_HALO + tm, :] = qk_ref[...]

    sel_r = lax.broadcasted_iota(jnp.int32, (L, CONV_HALO + L), 0)
    sel_c = lax.broadcasted_iota(jnp.int32, (L, CONV_HALO + L), 1)
    shift_sel = jnp.concatenate(
        [jnp.where(sel_c == sel_r + (CONV_HALO - (M_CONV - 1) + j), 1.0, 0.0)
         for j in range(M_CONV - 1)], axis=0).astype(BF16)

    row = lax.broadcasted_iota(jnp.int32, (L, L), 0)
    col = lax.broadcasted_iota(jnp.int32, (L, L), 1)
    tril = jnp.where(col <= row, 1.0, 0.0).astype(F32)
    key_le_query = row <= col
    sub8 = lax.broadcasted_iota(jnp.int32, (8, 1), 0)

    def gate_terms(c):
        gates = gate_ref[c * L:(c + 1) * L, :] + bias_ref[...]
        bcum = jnp.dot(tril, jax.nn.log_sigmoid(gates), precision=lax.Precision.HIGHEST,
                       preferred_element_type=F32)
        return gates, bcum, gates.T, bcum.T

    def hi_lo_rows(v):
        hi = v.astype(BF16).astype(F32)
        return jnp.where(sub8 == 0, hi, jnp.where(sub8 == 1, v - hi, 0.0)).astype(BF16)

    def conv_silu(c, shifted, c0, scale):
        r0 = CONV_HALO + c * L
        acc = (hist_ref[r0:r0 + L, c0:c0 + D].astype(F32)
               * conv_ref[M_CONV - 1:M_CONV, c0:c0 + D])
        for j in range(M_CONV - 1):
            acc = acc + shifted[j * L:(j + 1) * L, :] * conv_ref[j:j + 1, c0:c0 + D]
        half = acc.astype(BF16) * (0.5 * scale)
        arg = half if scale == 1.0 else half * (1.0 / scale)
        return half * jnp.tanh(arg) + half

    def shifted_rows(c, h):
        w0 = c * L
        return [_dot(shift_sel, hist_ref[w0:w0 + CONV_HALO + L, cc:cc + D])
                for cc in (h * D, M_WIDTH + h * D)]

    def conv_qk(c, h, shifted):
        qb = conv_silu(c, shifted[0], h * D, 1.0)
        kb = conv_silu(c, shifted[1], M_WIDTH + h * D, M_HEAD_DIM ** -0.5)
        return qb, kb, _dot_nt(kb, qb)

    def head_chain(h, all_terms):
        n_chunks = len(all_terms)
        shifted = shifted_rows(0, h)
        yield
        nxt = conv_qk(0, h, shifted)
        yield
        for c, terms in enumerate(all_terms):
            qb, kb, st = nxt
            inter_t = _dot_nt(ct_ref[h].astype(BF16), qb)
            qn2 = _dot_nt(hi_lo_rows(n_ref[h:h + 1, :]), qb)
            if c + 1 < n_chunks:
                shifted = shifted_rows(c + 1, h)
            yield
            finish = chunk_body(c, h, terms, qb, kb, st, inter_t, qn2)
            next(finish)
            yield
            if c + 1 < n_chunks:
                nxt = conv_qk(c + 1, h, shifted)
            next(finish, None)
            yield

    def chunk_body(c, h, terms, qb, kb, st, inter_t, qn2):
        gates, bcum, gates_t, bcum_t = terms
        c0 = h * D
        sr = h
        b_row = bcum_t[M_HEADS + h:M_HEADS + h + 1, :]
        i_row = gates_t[h:h + 1, :]
        c_col = gates[:, h:h + 1] - bcum[:, M_HEADS + h:M_HEADS + h + 1]
        m_prev = m_ref[sr:sr + 1, 0:1]
        g_row = b_row + m_prev
        dm_t = jnp.where(key_le_query, c_col + b_row, -jnp.inf)
        mj = jnp.maximum(g_row, jnp.max(dm_t, axis=0, keepdims=True))
        s_t = st * jnp.exp(dm_t - mj)
        vt = vo_ref[c * L:(c + 1) * L, c0:c0 + D].astype(F32).T
        intra_t = _dot(vt.astype(BF16), s_t.astype(BF16))
        b_last = b_row[:, L - 1:L]
        w_row = b_last - b_row + i_row
        m_new = jnp.maximum(b_last + m_prev, jnp.max(w_row, axis=1, keepdims=True))
        decay = jnp.exp(b_last + m_prev - m_new)
        wi_row = jnp.exp(w_row - m_new)
        ct_new = _dot((vt * wi_row).astype(BF16), kb)
        n_new2 = _dot(hi_lo_rows(wi_row), kb)
        yield
        wg = jnp.exp(g_row - mj)
        num_t = wg * inter_t + intra_t
        den = wg * (qn2[0:1, :] + qn2[1:2, :]) + jnp.sum(s_t, axis=0, keepdims=True)
        hh_t = num_t * (1.0 / jnp.maximum(jnp.abs(den), jnp.exp(-mj)))
        rs = lax.rsqrt(jnp.mean(hh_t * hh_t, axis=0, keepdims=True) + EPS)
        hn_t = hh_t * rs * ngb_ref[c0:c0 + D, :]
        og = vo_ref[c * L:(c + 1) * L, M_WIDTH + c0:M_WIDTH + c0 + D]
        sig = jnp.tanh(og * 0.5) * 0.5 + 0.5
        hm_ref[write_slot, c * L:(c + 1) * L, c0:c0 + D] = sig * hn_t.T.astype(BF16)
        ct_ref[sr] = decay * ct_ref[sr] + ct_new
        n_ref[sr:sr + 1, :] = decay * n_ref[sr:sr + 1, :] + n_new2[0:1, :] + n_new2[1:2, :]
        m_ref[sr:sr + 1, :] = jnp.broadcast_to(m_new, (1, LANES))

    def mix_chain():
        n_col = 2 * LANES
        parts = []
        for c0 in range(0, D_MODEL, n_col):
            part = _dot(hm_ref[read_slot], wo_ref[0:M_WIDTH, c0:c0 + n_col])
            yield
            parts.append(part + _dot(ha_ref[...], wo_ref[M_WIDTH:, c0:c0 + n_col]))
            yield
        h1 = x_ref[...] + _rms(jnp.concatenate(parts, axis=1), gpost_ref[...])
        h1_ref[...] = h1
        u = _rms(h1, gpre_ref[...]).astype(BF16)
        yield
        for c0 in range(0, D_MODEL, n_col):
            xq_ref[:, c0:c0 + n_col] = (_dot(u, wq_ref[:, c0:c0 + n_col])
                                        * (X_HEAD_DIM ** -0.5 * LOG2E)).astype(xq_ref.dtype)
            yield

    all_terms = [gate_terms(c) for c in range(tm // L)]
    chains = [head_chain(h, all_terms) for h in range(M_HEADS)] + [mix_chain()]
    _run_staggered(chains, start_per_round=len(chains))

    hist_ref[0:CONV_HALO, :] = hist_ref[tm:tm + CONV_HALO, :]


def _mlstm_mix(qk, vo, gates, conv_qk, gate_bias, norm_g_lanes, ha, x2, w_out, g_post, g_pre,
               w_xq, tm, seq):
    t = x2.shape[0]
    n_tiles = t // tm
    cur = lambda s: (jnp.minimum(s, n_tiles - 1), 0)
    prev = lambda s: (jnp.maximum(s - 1, 0), 0)
    return pl.pallas_call(
        functools.partial(_mlstm_mix_kernel, tiles_per_seq=seq // tm),
        grid=(n_tiles + 1,),
        in_specs=[pl.BlockSpec((tm, 2 * M_WIDTH), cur),
                  pl.BlockSpec((tm, 2 * M_WIDTH), cur),
                  pl.BlockSpec((tm, LANES), cur),
                  _const_spec((M_CONV, 2 * M_WIDTH)),
                  _const_spec((1, LANES)),
                  _const_spec((M_WIDTH, LANES)),
                  pl.BlockSpec((tm, A_WIDTH), prev),
                  pl.BlockSpec((tm, D_MODEL), prev),
                  _const_spec(w_out.shape),
                  _const_spec((1, D_MODEL)), _const_spec((1, D_MODEL)),
                  _const_spec(w_xq.shape)],
        out_specs=[pl.BlockSpec((tm, D_MODEL), prev), pl.BlockSpec((tm, D_MODEL), prev)],
        out_shape=[jax.ShapeDtypeStruct((t, D_MODEL), F32),
                   jax.ShapeDtypeStruct((t, D_MODEL), BF16)],
        scratch_shapes=[pltpu.VMEM((tm + CONV_HALO, 2 * M_WIDTH), BF16),
                        pltpu.VMEM((M_HEADS, M_HEAD_DIM, M_HEAD_DIM), F32),
                        pltpu.VMEM((M_HEADS, M_HEAD_DIM), F32),
                        pltpu.VMEM((M_HEADS, LANES), F32),
                        pltpu.VMEM((2, tm, M_WIDTH), BF16)],
        compiler_params=pltpu.CompilerParams(dimension_semantics=("arbitrary",),
                                             vmem_limit_bytes=VMEM_LIMIT),
        name="mlstm_mix",
    )(qk, vo, gates, conv_qk, gate_bias, norm_g_lanes, ha, x2, w_out, g_post, g_pre, w_xq)


def _mem_kv_kernel(mem_ref, g_ref, w_ref, kv_ref):
    u = _rms(mem_ref[...], g_ref[...]).astype(BF16)
    kv_ref[...] = _dot(u, w_ref[...]).astype(kv_ref.dtype)


def _mem_kv(mem2, g, w_xkv, tm):
    t = mem2.shape[0]
    assert t % tm == 0
    return pl.pallas_call(
        _mem_kv_kernel,
        grid=(t // tm,),
        in_specs=[pl.BlockSpec((tm, D_MODEL), lambda i: (i, 0)),
                  _const_spec((1, D_MODEL)),
                  _const_spec(w_xkv.shape)],
        out_specs=pl.BlockSpec((tm, 2 * D_MODEL), lambda i: (i, 0)),
        out_shape=jax.ShapeDtypeStruct((t, 2 * D_MODEL), BF16),
        compiler_params=pltpu.CompilerParams(dimension_semantics=("arbitrary",),
                                             vmem_limit_bytes=VMEM_LIMIT),
        name="mem_kv",
    )(mem2, g, w_xkv)


def _xattn_ffn_kernel(h1_ref, xq_ref, kv_ref, wo_ref, gxpost_ref, gfpre_ref, wgu_ref, wd_ref,
                      gfpost_ref, out_ref, o_ref, *, ff_chunk):
    d_ff = wd_ref.shape[0]
    chunks = [(j0, min(j0 + ff_chunk, d_ff)) for j0 in range(0, d_ff, ff_chunk)]

    def rows_stages(r0, r1):
        heads = [h * X_HEAD_DIM for h in range(X_HEADS)]
        scores = [_dot_nt(xq_ref[r0:r1, c0:c0 + X_HEAD_DIM], kv_ref[:, c0:c0 + X_HEAD_DIM])
                  for c0 in heads]
        yield
        for c0, s in zip(heads, scores):
            e = jnp.exp2(s - jnp.max(s, axis=1, keepdims=True))
            inv = 1.0 / jnp.sum(e, axis=1, keepdims=True)
            o = _dot(e.astype(BF16), kv_ref[:, D_MODEL + c0:D_MODEL + c0 + X_HEAD_DIM])
            o_ref[r0:r1, c0:c0 + X_HEAD_DIM] = (o * inv).astype(BF16)
        yield
        c = _dot(o_ref[r0:r1, :], wo_ref[...])
        yield
        h2 = h1_ref[r0:r1, :] + _rms(c, gxpost_ref[...])
        u = _rms(h2, gfpre_ref[...]).astype(BF16)
        f = None
        for j0, j1 in chunks:
            gate = _dot(u, wgu_ref[:, j0:j1])
            up = _dot(u, wgu_ref[:, d_ff + j0:d_ff + j1])
            yield
            act = (gate * jax.nn.sigmoid(gate) * up).astype(BF16)
            part = _dot(act, wd_ref[j0:j1, :])
            f = part if f is None else f + part
        yield
        out_ref[r0:r1, :] = h2 + _rms(f, gfpost_ref[...])

    tm = h1_ref.shape[0]
    step = tm // ROW_GROUPS
    _run_staggered([rows_stages(r0, r0 + step) for r0 in range(0, tm, step)],
                   start_per_round=ROW_GROUPS)


def _xattn_ffn(h1, xq, kv, w_xo, g_xpost, g_fpre, w_gu, w_d, g_fpost, tm, seq, mem_len, ff_chunk):
    t = h1.shape[0]
    row = lambda i: (i, 0)
    per_b = seq // tm
    return pl.pallas_call(
        functools.partial(_xattn_ffn_kernel, ff_chunk=ff_chunk),
        grid=(t // tm,),
        in_specs=[pl.BlockSpec((tm, D_MODEL), row),
                  pl.BlockSpec((tm, D_MODEL), row),
                  pl.BlockSpec((mem_len, 2 * D_MODEL), lambda i: (i // per_b, 0)),
                  _const_spec(w_xo.shape),
                  _const_spec((1, D_MODEL)), _const_spec((1, D_MODEL)),
                  _const_spec(w_gu.shape), _const_spec(w_d.shape),
                  _const_spec((1, D_MODEL))],
        out_specs=pl.BlockSpec((tm, D_MODEL), row),
        out_shape=jax.ShapeDtypeStruct((t, D_MODEL), F32),
        scratch_shapes=[pltpu.VMEM((tm, D_MODEL), BF16)],
        compiler_params=pltpu.CompilerParams(dimension_semantics=("arbitrary",),
                                             vmem_limit_bytes=VMEM_LIMIT),
        name="xattn_ffn",
    )(h1, xq, kv, w_xo, g_xpost, g_fpre, w_gu, w_d, g_fpost)


def _layer(h, mem2, pos2, inv_freq, p, batch, seq, mem_len):
    w_in = p["w_in"].astype(BF16)
    o_i = 4 * M_WIDTH
    o_qa = o_i + 2 * M_HEADS
    o_ka = o_qa + A_WIDTH
    w_gate = jnp.pad(w_in[:, o_i:o_qa], ((0, 0), (0, LANES - 2 * M_HEADS)))

    def pair_interleave(w):
        w5 = w.reshape(w.shape[0], -1, 2, 2, A_HEAD_DIM // 2)
        return jnp.swapaxes(w5, 2, 3).reshape(w.shape)

    w_cat = jnp.concatenate([w_in[:, :o_i], pair_interleave(w_in[:, o_qa:o_ka]),
                             pair_interleave(w_in[:, o_ka:o_ka + A_KV_WIDTH]),
                             w_in[:, o_ka + A_KV_WIDTH:], w_gate], axis=1)
    gate_bias = jnp.pad(jnp.concatenate([p["i_bias"], p["f_bias"]]),
                        (0, LANES - 2 * M_HEADS)).reshape(1, LANES)
    row = lambda v: v.reshape(1, -1)

    qk, vo, ha, gates = _in_proj(p["attn_sinks"], h, row(p["mix_pre_g"]), pos2, inv_freq, w_cat,
                                 tm=512, seq=seq)
    norm_g_lanes = jnp.broadcast_to(p["mlstm_norm_g"][:, None], (M_WIDTH, LANES))
    h1, xq = _mlstm_mix(qk, vo, gates, p["conv_qk"], gate_bias, norm_g_lanes, ha, h,
                        p["w_out"].astype(BF16), row(p["mix_post_g"]),
                        row(p["xattn_pre_g"]), p["w_xq"].astype(BF16), tm=512, seq=seq)
    kv = _mem_kv(mem2, row(p["mem_norm_g"]), p["w_xkv"].astype(BF16), tm=1024)
    return _xattn_ffn(h1, xq, kv, p["w_xo"].astype(BF16), row(p["xattn_post_g"]),
                      row(p["ffn_pre_g"]), p["w_gate_up"].astype(BF16), p["w_down"].astype(BF16),
                      row(p["ffn_post_g"]), tm=512, seq=seq, mem_len=mem_len, ff_chunk=1536)


def kernel(x, mem, positions, mix_pre_g, mix_post_g, w_in, conv_qk, f_bias, i_bias, mlstm_norm_g,
           attn_sinks, w_out, xattn_pre_g, xattn_post_g, mem_norm_g, w_xq, w_xkv, w_xo,
           ffn_pre_g, ffn_post_g, w_gate_up, w_down):
    batch, seq, _ = x.shape
    mem_len = mem.shape[1]
    depth = w_in.shape[0]
    params = dict(mix_pre_g=mix_pre_g, mix_post_g=mix_post_g, w_in=w_in, conv_qk=conv_qk,
                  f_bias=f_bias, i_bias=i_bias, mlstm_norm_g=mlstm_norm_g, attn_sinks=attn_sinks,
                  w_out=w_out, xattn_pre_g=xattn_pre_g, xattn_post_g=xattn_post_g,
                  mem_norm_g=mem_norm_g, w_xq=w_xq, w_xkv=w_xkv, w_xo=w_xo, ffn_pre_g=ffn_pre_g,
                  ffn_post_g=ffn_post_g, w_gate_up=w_gate_up, w_down=w_down)
    inv = ROPE_THETA ** (-jnp.arange(0, A_HEAD_DIM, 2, dtype=F32) / A_HEAD_DIM)
    inv_freq = jnp.tile(inv, LANES // inv.shape[0]).reshape(1, LANES)
    h = x.reshape(batch * seq, D_MODEL)
    mem2 = mem.reshape(batch * mem_len, D_MODEL)
    pos2 = positions.reshape(batch * seq, 1)
    for l in range(depth):
        h = _layer(h, mem2, pos2, inv_freq, {k: v[l] for k, v in params.items()},
                   batch, seq, mem_len)
    return h.reshape(batch, seq, D_MODEL)
```

```python
import functools

import jax
import jax.numpy as jnp
from jax import lax
from jax.experimental import pallas as pl
from jax.experimental.pallas import tpu as pltpu

D_MODEL = 1024
EPS = 1e-6
M_HEADS = 4
M_HEAD_DIM = 256
M_WIDTH = M_HEADS * M_HEAD_DIM
M_CONV = 4
M_CHUNK = 128
A_Q_HEADS = 16
A_KV_HEADS = 2
A_HEAD_DIM = 64
A_WIDTH = A_Q_HEADS * A_HEAD_DIM
A_KV_WIDTH = A_KV_HEADS * A_HEAD_DIM
WINDOW = 128
ROPE_THETA = 10000.0
LOG2E = 1.4426950408889634
X_HEADS = 4
X_HEAD_DIM = D_MODEL // X_HEADS
LANES = 128
SUBLANES = 8
N_KEY_WINDOWS = 2 * A_KV_HEADS
CONV_HALO = SUBLANES
ROW_GROUPS = 2
SCORE_LOOKAHEAD = 3
EARLY_STREAM_CHUNKS = 2
VMEM_LIMIT = 56 * 1024 * 1024

F32 = jnp.float32
BF16 = jnp.bfloat16


def _rms(x, g):
    return x * lax.rsqrt(jnp.mean(x * x, axis=-1, keepdims=True) + EPS) * g


def _dot(a, b):
    return jnp.dot(a, b, preferred_element_type=F32)


def _dot_nt(a, b):
    return lax.dot_general(a, b, (((1,), (1,)), ((), ())), preferred_element_type=F32)


def _run_staggered(chains, start_per_round=1):
    pending = list(chains)
    running = []
    while pending or running:
        for _ in range(min(start_per_round, len(pending))):
            running.append(pending.pop(0))
        for gen in list(running):
            if next(gen, "done") == "done":
                running.remove(gen)


def _const_spec(shape):
    nd = len(shape)
    return pl.BlockSpec(shape, lambda *_: (0,) * nd, pipeline_mode=pl.Buffered(1))


def _in_proj_kernel(sink_ref, x_ref, g_ref, pos_ref, inv_ref, conv_ref, w_ref, qk_ref, vo_ref,
                    ha_ref, gate_ref, q_scr, k_ref, vt_ref, halo_ref, *, tiles_per_seq):
    W = WINDOW
    tm = x_ref.shape[0]
    n_blk = tm // W
    half = A_HEAD_DIM // 2
    first_tile = pl.program_id(0) % tiles_per_seq == 0

    @pl.when(first_tile)
    def _():
        for s in range(N_KEY_WINDOWS):
            k_ref[0, s * 2 * W:s * 2 * W + W, :] = jnp.zeros((W, LANES), BF16)
        vt_ref[0, :, 0:W] = jnp.zeros((LANES, W), BF16)
        halo_ref[...] = jnp.zeros_like(halo_ref)

    @pl.when(jnp.logical_not(first_tile))
    def _():
        for s in range(N_KEY_WINDOWS):
            k_ref[0, s * 2 * W:s * 2 * W + W, :] = k_ref[n_blk, s * 2 * W:s * 2 * W + W, :]
        vt_ref[0, :, 0:W] = vt_ref[n_blk, :, 0:W]

    u = _rms(x_ref[...], g_ref[...]).astype(BF16)

    stream_chunks = [(ref, c0, base + c0)
                     for c0 in range(0, 2 * M_WIDTH, 2 * LANES)
                     for ref, base in ((qk_ref, 0), (vo_ref, 2 * M_WIDTH))]

    def project_chunk():
        ref, c0, wc = stream_chunks.pop(0)
        t = _dot(u, w_ref[:, wc:wc + 2 * LANES])
        if ref is vo_ref:
            ref[:, c0:c0 + 2 * LANES] = t.astype(ref.dtype)
            return
        ext = jnp.concatenate([halo_ref[:, c0:c0 + 2 * LANES], t], axis=0)
        acc = t * conv_ref[M_CONV - 1:M_CONV, c0:c0 + 2 * LANES]
        for s in range(1, M_CONV):
            shifted = pltpu.roll(ext, s, 0)[CONV_HALO:, :]
            acc = acc + shifted * conv_ref[M_CONV - 1 - s:M_CONV - s, c0:c0 + 2 * LANES]
        halo_ref[:, c0:c0 + 2 * LANES] = t[tm - CONV_HALO:, :]
        scale = M_HEAD_DIM ** -0.5 if c0 >= M_WIDTH else 1.0
        half_x = acc.astype(BF16) * (0.5 * scale)
        arg = half_x if scale == 1.0 else half_x * (1.0 / scale)
        ref[:, c0:c0 + 2 * LANES] = half_x * jnp.tanh(arg) + half_x

    for _ in range(EARLY_STREAM_CHUNKS):
        project_chunk()
    off = 4 * M_WIDTH

    n_freq = A_HEAD_DIM // 2
    n_seg = LANES // n_freq
    rows = tm // n_seg
    seg = lax.broadcasted_iota(jnp.int32, (rows, LANES), 1) // n_freq
    pos_f = pos_ref[...].astype(F32)
    pos_c = jnp.broadcast_to(pos_f[0:rows], (rows, LANES))
    for s in range(1, n_seg):
        pos_c = jnp.where(seg == s, pos_f[s * rows:(s + 1) * rows], pos_c)
    ang_c = pos_c * inv_ref[...]

    def spread(table_c):
        rolled = [table_c] + [pltpu.roll(table_c, n_freq * k, 1) for k in range(1, n_seg)]
        chunks = []
        for s in range(n_seg):
            full = rolled[(0 - s) % n_seg]
            for p in range(1, n_seg):
                full = jnp.where(seg == p, rolled[(p - s) % n_seg], full)
            chunks.append(full)
        return jnp.concatenate(chunks, axis=0)

    lane = lax.broadcasted_iota(jnp.int32, (tm, LANES), 1)
    cos_t = spread(jnp.cos(ang_c))
    sin_t = spread(jnp.sin(ang_c))
    sin_t = jnp.where(lane < LANES // 2, -sin_t, sin_t)

    def rope(t, c, s):
        return t * c + pltpu.roll(t, LANES // 2, 1) * s

    scale = A_HEAD_DIM ** -0.5 * LOG2E
    cos_q = cos_t * scale
    sin_q = sin_t * scale
    o_kv = off + A_WIDTH
    kv = _dot(u, w_ref[:, o_kv:o_kv + 2 * LANES])
    k_rot = rope(kv[:, 0:LANES], cos_t, sin_t)
    slot_a = (lane % A_HEAD_DIM) < half
    variants = (jnp.where(slot_a, k_rot, 0.0),
                jnp.where(slot_a, 0.0, pltpu.roll(k_rot, half, 1)),
                jnp.where(slot_a, pltpu.roll(k_rot, LANES - half, 1), 0.0),
                jnp.where(slot_a, 0.0, k_rot))
    for j in range(n_blk):
        for s, var in enumerate(variants):
            blk = var[j * W:(j + 1) * W, :].astype(BF16)
            k_ref[j, s * 2 * W + W:(s + 1) * 2 * W, :] = blk
            k_ref[j + 1, s * 2 * W:s * 2 * W + W, :] = blk
        vt = kv[j * W:(j + 1) * W, LANES:2 * LANES].T.astype(BF16)
        vt_ref[j, :, W:2 * W] = vt
        vt_ref[j + 1, :, 0:W] = vt

    for j in range(A_WIDTH // (2 * LANES)):
        t = _dot(u, w_ref[:, off:off + 2 * LANES])
        for i in range(2):
            c0 = (2 * j + i) * LANES
            q_scr[:, c0:c0 + LANES] = rope(t[:, i * LANES:(i + 1) * LANES],
                                           cos_q, sin_q).astype(q_scr.dtype)
        off += 2 * LANES
    gate_ref[...] = _dot(u, w_ref[:, o_kv + 2 * LANES:o_kv + 3 * LANES])

    kpos = lax.broadcasted_iota(jnp.int32, (2 * W, W), 0)
    qpos = lax.broadcasted_iota(jnp.int32, (2 * W, W), 1)
    diff = W + qpos - kpos
    band = (diff >= 0) & (diff < W)
    bias_rest = jnp.where(band, 0.0, -jnp.inf).astype(F32)
    bias_first = jnp.where(band & (jnp.logical_not(first_tile) | (kpos >= W)),
                           0.0, -jnp.inf).astype(F32)

    n_pairs = A_Q_HEADS // 2
    pairs_per_group = n_pairs // A_KV_HEADS
    items = [(j, pair) for j in range(n_blk) for pair in range(n_pairs)]

    def scores(j, pair):
        g = pair // pairs_per_group
        q = q_scr[j * W:(j + 1) * W, pair * LANES:(pair + 1) * LANES]
        return _dot_nt(k_ref[j, g * 4 * W:(g + 1) * 4 * W, :], q)

    n_late = len(stream_chunks)
    pending = [scores(*it) for it in items[:SCORE_LOOKAHEAD]]
    for idx, (j, pair) in enumerate(items):
        g = pair // pairs_per_group
        if idx + SCORE_LOOKAHEAD < len(items):
            pending.append(scores(*items[idx + SCORE_LOOKAHEAD]))
        st_pair = pending.pop(0)
        bias = bias_first if j == 0 else bias_rest
        pes, inv_dens = [], []
        for slot in range(2):
            sink = sink_ref[2 * pair + slot] * LOG2E
            st = st_pair[slot * 2 * W:(slot + 1) * 2 * W, :] + bias
            mx = jnp.maximum(jnp.max(st, axis=0, keepdims=True), sink)
            pe = jnp.exp2(st - mx)
            den = jnp.sum(pe, axis=0, keepdims=True) + jnp.exp2(sink - mx)
            pes.append(pe.astype(BF16))
            inv_dens.append(1.0 / den)
        ot = _dot(vt_ref[j, g * A_HEAD_DIM:(g + 1) * A_HEAD_DIM, :],
                  jnp.concatenate(pes, axis=1))
        o_pair = jnp.concatenate([ot[:, slot * W:(slot + 1) * W] * inv_dens[slot]
                                  for slot in range(2)], axis=0)
        ha_ref[j * W:(j + 1) * W, pair * LANES:(pair + 1) * LANES] = (
            o_pair.T.astype(ha_ref.dtype))
        if (idx + 1) * n_late // len(items) > idx * n_late // len(items):
            project_chunk()
    assert not stream_chunks


def _in_proj(sinks, x2, g, pos, inv_freq, conv_qk, w_cat, tm, seq):
    t = x2.shape[0]
    widths = (2 * M_WIDTH, 2 * M_WIDTH, A_WIDTH, LANES)
    dtypes = (BF16, BF16, BF16, F32)
    n_blk = tm // WINDOW
    return pl.pallas_call(
        functools.partial(_in_proj_kernel, tiles_per_seq=seq // tm),
        grid=(t // tm,),
        in_specs=[pl.BlockSpec(memory_space=pltpu.SMEM),
                  pl.BlockSpec((tm, D_MODEL), lambda i: (i, 0)),
                  _const_spec((1, D_MODEL)),
                  pl.BlockSpec((tm, 1), lambda i: (i, 0)),
                  _const_spec((1, LANES)),
                  _const_spec(conv_qk.shape),
                  _const_spec(w_cat.shape)],
        out_specs=[pl.BlockSpec((tm, w), lambda i: (i, 0)) for w in widths],
        out_shape=[jax.ShapeDtypeStruct((t, w), d) for w, d in zip(widths, dtypes)],
        scratch_shapes=[pltpu.VMEM((tm, A_WIDTH), BF16),
                        pltpu.VMEM((n_blk + 1, N_KEY_WINDOWS * 2 * WINDOW, LANES), BF16),
                        pltpu.VMEM((n_blk + 1, LANES, 2 * WINDOW), BF16),
                        pltpu.VMEM((CONV_HALO, 2 * M_WIDTH), F32)],
        compiler_params=pltpu.CompilerParams(dimension_semantics=("arbitrary",),
                                             vmem_limit_bytes=VMEM_LIMIT),
        name="in_proj",
    )(sinks, x2, g, pos, inv_freq, conv_qk, w_cat)


def _mlstm_mix_kernel(qk_ref, vo_ref, gate_ref, bias_ref, ngb_ref, ha_ref, x_ref,
                      wo_ref, gpost_ref, gpre_ref, wq_ref, h1_ref, xq_ref,
                      ct_ref, n_ref, m_ref, hm_ref, *, tiles_per_seq):
    L = M_CHUNK
    D = M_HEAD_DIM
    tm = qk_ref.shape[0]
    step = pl.program_id(0)
    write_slot = step % 2
    read_slot = (step + 1) % 2

    @pl.when(step % tiles_per_seq == 0)
    def _():
        ct_ref[...] = jnp.zeros_like(ct_ref)
        n_ref[...] = jnp.zeros_like(n_ref)
        m_ref[...] = jnp.zeros_like(m_ref)

    @pl.when(step == 0)
    def _():
        hm_ref[1] = jnp.zeros(hm_ref.shape[1:], hm_ref.dtype)

    row = lax.broadcasted_iota(jnp.int32, (L, L), 0)
    col = lax.broadcasted_iota(jnp.int32, (L, L), 1)
    tril = jnp.where(col <= row, 1.0, 0.0).astype(F32)
    key_le_query = row <= col
    sub8 = lax.broadcasted_iota(jnp.int32, (SUBLANES, 1), 0)

    def gate_terms(c):
        gates = gate_ref[c * L:(c + 1) * L, :] + bias_ref[...]
        bcum = jnp.dot(tril, jax.nn.log_sigmoid(gates), precision=lax.Precision.HIGHEST,
                       preferred_element_type=F32)
        return gates, bcum, gates.T, bcum.T

    def hi_lo_rows(v):
        hi = v.astype(BF16).astype(F32)
        return jnp.where(sub8 == 0, hi, jnp.where(sub8 == 1, v - hi, 0.0)).astype(BF16)

    def load_qk(c, h):
        qb = qk_ref[c * L:(c + 1) * L, h * D:(h + 1) * D]
        kb = qk_ref[c * L:(c + 1) * L, M_WIDTH + h * D:M_WIDTH + (h + 1) * D]
        return qb, kb, _dot_nt(kb, qb)

    def head_chain(h, all_terms):
        n_chunks = len(all_terms)
        nxt = load_qk(0, h)
        yield
        for c, terms in enumerate(all_terms):
            qb, kb, st = nxt
            inter_t = _dot_nt(ct_ref[h].astype(BF16), qb)
            qn2 = _dot_nt(hi_lo_rows(n_ref[h:h + 1, :]), qb)
            yield
            finish = chunk_body(c, h, terms, qb, kb, st, inter_t, qn2)
            next(finish)
            if c + 1 < n_chunks:
                nxt = load_qk(c + 1, h)
            yield
            next(finish, None)
            yield

    def chunk_body(c, h, terms, qb, kb, st, inter_t, qn2):
        gates, bcum, gates_t, bcum_t = terms
        c0 = h * D
        sr = h
        b_row = bcum_t[M_HEADS + h:M_HEADS + h + 1, :]
        i_row = gates_t[h:h + 1, :]
        c_col = gates[:, h:h + 1] - bcum[:, M_HEADS + h:M_HEADS + h + 1]
        m_prev = m_ref[sr:sr + 1, 0:1]
        g_row = b_row + m_prev
        dm_t = jnp.where(key_le_query, c_col + b_row, -jnp.inf)
        mj = jnp.maximum(g_row, jnp.max(dm_t, axis=0, keepdims=True))
        s_t = st * jnp.exp(dm_t - mj)
        vt = vo_ref[c * L:(c + 1) * L, c0:c0 + D].astype(F32).T
        intra_t = _dot(vt.astype(BF16), s_t.astype(BF16))
        b_last = b_row[:, L - 1:L]
        w_row = b_last - b_row + i_row
        m_new = jnp.maximum(b_last + m_prev, jnp.max(w_row, axis=1, keepdims=True))
        decay = jnp.exp(b_last + m_prev - m_new)
        wi_row = jnp.exp(w_row - m_new)
        ct_new = _dot((vt * wi_row).astype(BF16), kb)
        n_new2 = _dot(hi_lo_rows(wi_row), kb)
        yield
        wg = jnp.exp(g_row - mj)
        num_t = wg * inter_t + intra_t
        den = wg * (qn2[0:1, :] + qn2[1:2, :]) + jnp.sum(s_t, axis=0, keepdims=True)
        hh_t = num_t * (1.0 / jnp.maximum(jnp.abs(den), jnp.exp(-mj)))
        rs = lax.rsqrt(jnp.mean(hh_t * hh_t, axis=0, keepdims=True) + EPS)
        hn_t = hh_t * rs * ngb_ref[c0:c0 + D, :]
        og = vo_ref[c * L:(c + 1) * L, M_WIDTH + c0:M_WIDTH + c0 + D]
        sig = jnp.tanh(og * 0.5) * 0.5 + 0.5
        hm_ref[write_slot, c * L:(c + 1) * L, c0:c0 + D] = sig * hn_t.T.astype(BF16)
        ct_ref[sr] = decay * ct_ref[sr] + ct_new
        n_ref[sr:sr + 1, :] = decay * n_ref[sr:sr + 1, :] + n_new2[0:1, :] + n_new2[1:2, :]
        m_ref[sr:sr + 1, :] = jnp.broadcast_to(m_new, (1, LANES))

    def mix_chain():
        n_col = 2 * LANES
        parts = []
        for c0 in range(0, D_MODEL, n_col):
            part = _dot(hm_ref[read_slot], wo_ref[0:M_WIDTH, c0:c0 + n_col])
            yield
            parts.append(part + _dot(ha_ref[...], wo_ref[M_WIDTH:, c0:c0 + n_col]))
            yield
        h1 = x_ref[...] + _rms(jnp.concatenate(parts, axis=1), gpost_ref[...])
        h1_ref[...] = h1
        u = _rms(h1, gpre_ref[...]).astype(BF16)
        yield
        for c0 in range(0, D_MODEL, n_col):
            xq_ref[:, c0:c0 + n_col] = (_dot(u, wq_ref[:, c0:c0 + n_col])
                                        * (X_HEAD_DIM ** -0.5 * LOG2E)).astype(xq_ref.dtype)
            yield

    all_terms = [gate_terms(c) for c in range(tm // L)]
    chains = [head_chain(h, all_terms) for h in range(M_HEADS)] + [mix_chain()]
    _run_staggered(chains, start_per_round=len(chains))


def _mlstm_mix(qk, vo, gates, gate_bias, norm_g_lanes, ha, x2, w_out, g_post, g_pre, w_xq,
               tm, seq):
    t = x2.shape[0]
    n_tiles = t // tm
    cur = lambda s: (jnp.minimum(s, n_tiles - 1), 0)
    prev = lambda s: (jnp.maximum(s - 1, 0), 0)
    return pl.pallas_call(
        functools.partial(_mlstm_mix_kernel, tiles_per_seq=seq // tm),
        grid=(n_tiles + 1,),
        in_specs=[pl.BlockSpec((tm, 2 * M_WIDTH), cur),
                  pl.BlockSpec((tm, 2 * M_WIDTH), cur),
                  pl.BlockSpec((tm, LANES), cur),
                  _const_spec((1, LANES)),
                  _const_spec((M_WIDTH, LANES)),
                  pl.BlockSpec((tm, A_WIDTH), prev),
                  pl.BlockSpec((tm, D_MODEL), prev),
                  _const_spec(w_out.shape),
                  _const_spec((1, D_MODEL)), _const_spec((1, D_MODEL)),
                  _const_spec(w_xq.shape)],
        out_specs=[pl.BlockSpec((tm, D_MODEL), prev), pl.BlockSpec((tm, D_MODEL), prev)],
        out_shape=[jax.ShapeDtypeStruct((t, D_MODEL), F32),
                   jax.ShapeDtypeStruct((t, D_MODEL), BF16)],
        scratch_shapes=[pltpu.VMEM((M_HEADS, M_HEAD_DIM, M_HEAD_DIM), F32),
                        pltpu.VMEM((M_HEADS, M_HEAD_DIM), F32),
                        pltpu.VMEM((M_HEADS, LANES), F32),
                        pltpu.VMEM((2, tm, M_WIDTH), BF16)],
        compiler_params=pltpu.CompilerParams(dimension_semantics=("arbitrary",),
                                             vmem_limit_bytes=VMEM_LIMIT),
        name="mlstm_mix",
    )(qk, vo, gates, gate_bias, norm_g_lanes, ha, x2, w_out, g_post, g_pre, w_xq)


def _mem_kv_kernel(mem_ref, g_ref, w_ref, kv_ref):
    u = _rms(mem_ref[...], g_ref[...]).astype(BF16)
    kv_ref[...] = _dot(u, w_ref[...]).astype(kv_ref.dtype)


def _mem_kv(mem2, g, w_xkv, tm):
    t = mem2.shape[0]
    assert t % tm == 0
    return pl.pallas_call(
        _mem_kv_kernel,
        grid=(t // tm,),
        in_specs=[pl.BlockSpec((tm, D_MODEL), lambda i: (i, 0)),
                  _const_spec((1, D_MODEL)),
                  _const_spec(w_xkv.shape)],
        out_specs=pl.BlockSpec((tm, 2 * D_MODEL), lambda i: (i, 0)),
        out_shape=jax.ShapeDtypeStruct((t, 2 * D_MODEL), BF16),
        compiler_params=pltpu.CompilerParams(dimension_semantics=("arbitrary",),
                                             vmem_limit_bytes=VMEM_LIMIT),
        name="mem_kv",
    )(mem2, g, w_xkv)


def _xattn_ffn_kernel(h1_ref, xq_ref, kv_ref, wo_ref, gxpost_ref, gfpre_ref, wgu_ref, wd_ref,
                      gfpost_ref, out_ref, o_ref, *, ff_chunk):
    d_ff = wd_ref.shape[0]
    chunks = [(j0, min(j0 + ff_chunk, d_ff)) for j0 in range(0, d_ff, ff_chunk)]

    def rows_stages(r0, r1):
        heads = [h * X_HEAD_DIM for h in range(X_HEADS)]
        scores = [_dot_nt(xq_ref[r0:r1, c0:c0 + X_HEAD_DIM], kv_ref[:, c0:c0 + X_HEAD_DIM])
                  for c0 in heads]
        yield
        for c0, s in zip(heads, scores):
            e = jnp.exp2(s - jnp.max(s, axis=1, keepdims=True))
            inv = 1.0 / jnp.sum(e, axis=1, keepdims=True)
            o = _dot(e.astype(BF16), kv_ref[:, D_MODEL + c0:D_MODEL + c0 + X_HEAD_DIM])
            o_ref[r0:r1, c0:c0 + X_HEAD_DIM] = (o * inv).astype(BF16)
        yield
        c = _dot(o_ref[r0:r1, :], wo_ref[...])
        yield
        h2 = h1_ref[r0:r1, :] + _rms(c, gxpost_ref[...])
        u = _rms(h2, gfpre_ref[...]).astype(BF16)
        f = None
        for j0, j1 in chunks:
            gate = _dot(u, wgu_ref[:, j0:j1])
            up = _dot(u, wgu_ref[:, d_ff + j0:d_ff + j1])
            yield
            act = (gate * jax.nn.sigmoid(gate) * up).astype(BF16)
            part = _dot(act, wd_ref[j0:j1, :])
            f = part if f is None else f + part
        yield
        out_ref[r0:r1, :] = h2 + _rms(f, gfpost_ref[...])

    tm = h1_ref.shape[0]
    step = tm // ROW_GROUPS
    _run_staggered([rows_stages(r0, r0 + step) for r0 in range(0, tm, step)],
                   start_per_round=ROW_GROUPS)


def _xattn_ffn(h1, xq, kv, w_xo, g_xpost, g_fpre, w_gu, w_d, g_fpost, tm, seq, mem_len, ff_chunk):
    t = h1.shape[0]
    row = lambda i: (i, 0)
    per_b = seq // tm
    return pl.pallas_call(
        functools.partial(_xattn_ffn_kernel, ff_chunk=ff_chunk),
        grid=(t // tm,),
        in_specs=[pl.BlockSpec((tm, D_MODEL), row),
                  pl.BlockSpec((tm, D_MODEL), row),
                  pl.BlockSpec((mem_len, 2 * D_MODEL), lambda i: (i // per_b, 0)),
                  _const_spec(w_xo.shape),
                  _const_spec((1, D_MODEL)), _const_spec((1, D_MODEL)),
                  _const_spec(w_gu.shape), _const_spec(w_d.shape),
                  _const_spec((1, D_MODEL))],
        out_specs=pl.BlockSpec((tm, D_MODEL), row),
        out_shape=jax.ShapeDtypeStruct((t, D_MODEL), F32),
        scratch_shapes=[pltpu.VMEM((tm, D_MODEL), BF16)],
        compiler_params=pltpu.CompilerParams(dimension_semantics=("arbitrary",),
                                             vmem_limit_bytes=VMEM_LIMIT),
        name="xattn_ffn",
    )(h1, xq, kv, w_xo, g_xpost, g_fpre, w_gu, w_d, g_fpost)


def _layer(h, mem2, pos2, inv_freq, p, batch, seq, mem_len):
    w_in = p["w_in"].astype(BF16)
    o_i = 4 * M_WIDTH
    o_qa = o_i + 2 * M_HEADS
    o_ka = o_qa + A_WIDTH
    w_gate = jnp.pad(w_in[:, o_i:o_qa], ((0, 0), (0, LANES - 2 * M_HEADS)))

    def pair_interleave(w):
        w5 = w.reshape(w.shape[0], -1, 2, 2, A_HEAD_DIM // 2)
        return jnp.swapaxes(w5, 2, 3).reshape(w.shape)

    w_cat = jnp.concatenate([w_in[:, :o_i], pair_interleave(w_in[:, o_qa:o_ka]),
                             pair_interleave(w_in[:, o_ka:o_ka + A_KV_WIDTH]),
                             w_in[:, o_ka + A_KV_WIDTH:], w_gate], axis=1)
    gate_bias = jnp.pad(jnp.concatenate([p["i_bias"], p["f_bias"]]),
                        (0, LANES - 2 * M_HEADS)).reshape(1, LANES)
    row = lambda v: v.reshape(1, -1)

    qk, vo, ha, gates = _in_proj(p["attn_sinks"], h, row(p["mix_pre_g"]), pos2, inv_freq,
                                 p["conv_qk"], w_cat, tm=512, seq=seq)
    norm_g_lanes = jnp.broadcast_to(p["mlstm_norm_g"][:, None], (M_WIDTH, LANES))
    h1, xq = _mlstm_mix(qk, vo, gates, gate_bias, norm_g_lanes, ha, h,
                        p["w_out"].astype(BF16), row(p["mix_post_g"]),
                        row(p["xattn_pre_g"]), p["w_xq"].astype(BF16), tm=512, seq=seq)
    kv = _mem_kv(mem2, row(p["mem_norm_g"]), p["w_xkv"].astype(BF16), tm=1024)
    return _xattn_ffn(h1, xq, kv, p["w_xo"].astype(BF16), row(p["xattn_post_g"]),
                      row(p["ffn_pre_g"]), p["w_gate_up"].astype(BF16), p["w_down"].astype(BF16),
                      row(p["ffn_post_g"]), tm=512, seq=seq, mem_len=mem_len, ff_chunk=1536)


def kernel(x, mem, positions, mix_pre_g, mix_post_g, w_in, conv_qk, f_bias, i_bias, mlstm_norm_g,
           attn_sinks, w_out, xattn_pre_g, xattn_post_g, mem_norm_g, w_xq, w_xkv, w_xo,
           ffn_pre_g, ffn_post_g, w_gate_up, w_down):
    batch, seq, _ = x.shape
    mem_len = mem.shape[1]
    depth = w_in.shape[0]
    params = dict(mix_pre_g=mix_pre_g, mix_post_g=mix_post_g, w_in=w_in, conv_qk=conv_qk,
                  f_bias=f_bias, i_bias=i_bias, mlstm_norm_g=mlstm_norm_g, attn_sinks=attn_sinks,
                  w_out=w_out, xattn_pre_g=xattn_pre_g, xattn_post_g=xattn_post_g,
                  mem_norm_g=mem_norm_g, w_xq=w_xq, w_xkv=w_xkv, w_xo=w_xo, ffn_pre_g=ffn_pre_g,
                  ffn_post_g=ffn_post_g, w_gate_up=w_gate_up, w_down=w_down)
    inv = ROPE_THETA ** (-jnp.arange(0, A_HEAD_DIM, 2, dtype=F32) / A_HEAD_DIM)
    inv_freq = jnp.tile(inv, LANES // inv.shape[0]).reshape(1, LANES)
    h = x.reshape(batch * seq, D_MODEL)
    mem2 = mem.reshape(batch * mem_len, D_MODEL)
    pos2 = positions.reshape(batch * seq, 1)
    for l in range(depth):
        h = _layer(h, mem2, pos2, inv_freq, {k: v[l] for k, v in params.items()},
                   batch, seq, mem_len)
    return h.reshape(batch, seq, D_MODEL)
```

```python
import functools

import jax
import jax.numpy as jnp
from jax import lax
from jax.experimental import pallas as pl
from jax.experimental.pallas import tpu as pltpu

D_MODEL = 1024
EPS = 1e-6
M_HEADS = 4
M_HEAD_DIM = 256
M_WIDTH = M_HEADS * M_HEAD_DIM
M_CONV = 4
M_CHUNK = 128
A_Q_HEADS = 16
A_KV_HEADS = 2
A_HEAD_DIM = 64
A_WIDTH = A_Q_HEADS * A_HEAD_DIM
A_KV_WIDTH = A_KV_HEADS * A_HEAD_DIM
WINDOW = 128
ROPE_THETA = 10000.0
LOG2E = 1.4426950408889634
X_HEADS = 4
X_HEAD_DIM = D_MODEL // X_HEADS
LANES = 128
SUBLANES = 8
N_KEY_WINDOWS = 2 * A_KV_HEADS
CONV_HALO = SUBLANES
ROW_GROUPS = 2
SCORE_LOOKAHEAD = 3
EARLY_STREAM_CHUNKS = 2
VMEM_LIMIT = 56 * 1024 * 1024

F32 = jnp.float32
BF16 = jnp.bfloat16


def _rms(x, g):
    return x * lax.rsqrt(jnp.mean(x * x, axis=-1, keepdims=True) + EPS) * g


def _dot(a, b):
    return jnp.dot(a, b, preferred_element_type=F32)


def _dot_nt(a, b):
    return lax.dot_general(a, b, (((1,), (1,)), ((), ())), preferred_element_type=F32)


def _run_staggered(chains, start_per_round=1):
    pending = list(chains)
    running = []
    while pending or running:
        for _ in range(min(start_per_round, len(pending))):
            running.append(pending.pop(0))
        for gen in list(running):
            if next(gen, "done") == "done":
                running.remove(gen)


def _const_spec(shape):
    nd = len(shape)
    return pl.BlockSpec(shape, lambda *_: (0,) * nd, pipeline_mode=pl.Buffered(1))


def _in_proj_kernel(sink_ref, x_ref, g_ref, pos_ref, inv_ref, conv_ref, w_ref, qk_ref, vo_ref,
                    ha_ref, gate_ref, q_scr, k_ref, vt_ref, halo_ref, *, tiles_per_seq):
    W = WINDOW
    tm = x_ref.shape[0]
    n_blk = tm // W
    half = A_HEAD_DIM // 2
    first_tile = pl.program_id(0) % tiles_per_seq == 0

    @pl.when(first_tile)
    def _():
        for s in range(N_KEY_WINDOWS):
            k_ref[0, s * 2 * W:s * 2 * W + W, :] = jnp.zeros((W, LANES), BF16)
        vt_ref[0, :, 0:W] = jnp.zeros((LANES, W), BF16)
        halo_ref[...] = jnp.zeros_like(halo_ref)

    @pl.when(jnp.logical_not(first_tile))
    def _():
        for s in range(N_KEY_WINDOWS):
            k_ref[0, s * 2 * W:s * 2 * W + W, :] = k_ref[n_blk, s * 2 * W:s * 2 * W + W, :]
        vt_ref[0, :, 0:W] = vt_ref[n_blk, :, 0:W]

    u = _rms(x_ref[...], g_ref[...]).astype(BF16)

    stream_chunks = [(ref, c0, base + c0)
                     for c0 in range(0, 2 * M_WIDTH, 2 * LANES)
                     for ref, base in ((qk_ref, 0), (vo_ref, 2 * M_WIDTH))]

    def project_chunk():
        ref, c0, wc = stream_chunks.pop(0)
        t = _dot(u, w_ref[:, wc:wc + 2 * LANES])
        if ref is vo_ref:
            ref[:, c0:c0 + 2 * LANES] = t.astype(ref.dtype)
            return
        assert M_CONV == 4
        ext = jnp.concatenate([halo_ref[:, c0:c0 + 2 * LANES], t], axis=0)
        prev1 = pltpu.roll(ext, 1, 0)
        taps = [conv_ref[j:j + 1, c0:c0 + 2 * LANES] for j in range(M_CONV)]
        older = ext * taps[1] + prev1 * taps[0]
        acc = (t * taps[3] + prev1[CONV_HALO:, :] * taps[2]
               + pltpu.roll(older, 2, 0)[CONV_HALO:, :])
        halo_ref[:, c0:c0 + 2 * LANES] = t[tm - CONV_HALO:, :]
        scale = M_HEAD_DIM ** -0.5 if c0 >= M_WIDTH else 1.0
        half_x = acc.astype(BF16) * (0.5 * scale)
        arg = half_x if scale == 1.0 else half_x * (1.0 / scale)
        ref[:, c0:c0 + 2 * LANES] = half_x * jnp.tanh(arg) + half_x

    for _ in range(EARLY_STREAM_CHUNKS):
        project_chunk()
    off = 4 * M_WIDTH

    n_freq = A_HEAD_DIM // 2
    n_seg = LANES // n_freq
    rows = tm // n_seg
    seg = lax.broadcasted_iota(jnp.int32, (rows, LANES), 1) // n_freq
    pos_f = pos_ref[...].astype(F32)
    pos_c = jnp.broadcast_to(pos_f[0:rows], (rows, LANES))
    for s in range(1, n_seg):
        pos_c = jnp.where(seg == s, pos_f[s * rows:(s + 1) * rows], pos_c)
    ang_c = pos_c * inv_ref[...]

    def spread(table_c):
        rolled = [table_c] + [pltpu.roll(table_c, n_freq * k, 1) for k in range(1, n_seg)]
        chunks = []
        for s in range(n_seg):
            full = rolled[(0 - s) % n_seg]
            for p in range(1, n_seg):
                full = jnp.where(seg == p, rolled[(p - s) % n_seg], full)
            chunks.append(full)
        return jnp.concatenate(chunks, axis=0)

    lane = lax.broadcasted_iota(jnp.int32, (tm, LANES), 1)
    cos_t = spread(jnp.cos(ang_c))
    sin_t = spread(jnp.sin(ang_c))
    sin_t = jnp.where(lane < LANES // 2, -sin_t, sin_t)

    def rope(t, c, s):
        return t * c + pltpu.roll(t, LANES // 2, 1) * s

    scale = A_HEAD_DIM ** -0.5 * LOG2E
    cos_q = cos_t * scale
    sin_q = sin_t * scale
    o_kv = off + A_WIDTH
    kv = _dot(u, w_ref[:, o_kv:o_kv + 2 * LANES])
    k_rot = rope(kv[:, 0:LANES], cos_t, sin_t)
    slot_a = (lane % A_HEAD_DIM) < half
    variants = (jnp.where(slot_a, k_rot, 0.0),
                jnp.where(slot_a, 0.0, pltpu.roll(k_rot, half, 1)),
                jnp.where(slot_a, pltpu.roll(k_rot, LANES - half, 1), 0.0),
                jnp.where(slot_a, 0.0, k_rot))
    for j in range(n_blk):
        for s, var in enumerate(variants):
            blk = var[j * W:(j + 1) * W, :].astype(BF16)
            k_ref[j, s * 2 * W + W:(s + 1) * 2 * W, :] = blk
            k_ref[j + 1, s * 2 * W:s * 2 * W + W, :] = blk
        vt = kv[j * W:(j + 1) * W, LANES:2 * LANES].T.astype(BF16)
        vt_ref[j, :, W:2 * W] = vt
        vt_ref[j + 1, :, 0:W] = vt

    for j in range(A_WIDTH // (2 * LANES)):
        t = _dot(u, w_ref[:, off:off + 2 * LANES])
        for i in range(2):
            c0 = (2 * j + i) * LANES
            q_scr[:, c0:c0 + LANES] = rope(t[:, i * LANES:(i + 1) * LANES],
                                           cos_q, sin_q).astype(q_scr.dtype)
        off += 2 * LANES
    gate_ref[...] = _dot(u, w_ref[:, o_kv + 2 * LANES:o_kv + 3 * LANES])

    kpos = lax.broadcasted_iota(jnp.int32, (2 * W, W), 0)
    qpos = lax.broadcasted_iota(jnp.int32, (2 * W, W), 1)
    diff = W + qpos - kpos
    band = (diff >= 0) & (diff < W)
    bias_rest = jnp.where(band, 0.0, -jnp.inf).astype(F32)
    bias_first = jnp.where(band & (jnp.logical_not(first_tile) | (kpos >= W)),
                           0.0, -jnp.inf).astype(F32)

    n_pairs = A_Q_HEADS // 2
    pairs_per_group = n_pairs // A_KV_HEADS
    items = [(j, pair) for j in range(n_blk) for pair in range(n_pairs)]

    def scores(j, pair):
        g = pair // pairs_per_group
        q = q_scr[j * W:(j + 1) * W, pair * LANES:(pair + 1) * LANES]
        return _dot_nt(k_ref[j, g * 4 * W:(g + 1) * 4 * W, :], q)

    n_late = len(stream_chunks)
    pending = [scores(*it) for it in items[:SCORE_LOOKAHEAD]]
    for idx, (j, pair) in enumerate(items):
        g = pair // pairs_per_group
        if idx + SCORE_LOOKAHEAD < len(items):
            pending.append(scores(*items[idx + SCORE_LOOKAHEAD]))
        st_pair = pending.pop(0)
        bias = bias_first if j == 0 else bias_rest
        pes, inv_dens = [], []
        for slot in range(2):
            sink = sink_ref[2 * pair + slot] * LOG2E
            st = st_pair[slot * 2 * W:(slot + 1) * 2 * W, :] + bias
            mx = jnp.maximum(jnp.max(st, axis=0, keepdims=True), sink)
            pe = jnp.exp2(st - mx)
            den = jnp.sum(pe, axis=0, keepdims=True) + jnp.exp2(sink - mx)
            pes.append(pe.astype(BF16))
            inv_dens.append(1.0 / den)
        ot = _dot(vt_ref[j, g * A_HEAD_DIM:(g + 1) * A_HEAD_DIM, :],
                  jnp.concatenate(pes, axis=1))
        o_pair = jnp.concatenate([ot[:, slot * W:(slot + 1) * W] * inv_dens[slot]
                                  for slot in range(2)], axis=0)
        ha_ref[j * W:(j + 1) * W, pair * LANES:(pair + 1) * LANES] = (
            o_pair.T.astype(ha_ref.dtype))
        if (idx + 1) * n_late // len(items) > idx * n_late // len(items):
            project_chunk()
    assert not stream_chunks


def _in_proj(sinks, x2, g, pos, inv_freq, conv_qk, w_cat, tm, seq):
    t = x2.shape[0]
    widths = (2 * M_WIDTH, 2 * M_WIDTH, A_WIDTH, LANES)
    dtypes = (BF16, BF16, BF16, F32)
    n_blk = tm // WINDOW
    return pl.pallas_call(
        functools.partial(_in_proj_kernel, tiles_per_seq=seq // tm),
        grid=(t // tm,),
        in_specs=[pl.BlockSpec(memory_space=pltpu.SMEM),
                  pl.BlockSpec((tm, D_MODEL), lambda i: (i, 0)),
                  _const_spec((1, D_MODEL)),
                  pl.BlockSpec((tm, 1), lambda i: (i, 0)),
                  _const_spec((1, LANES)),
                  _const_spec(conv_qk.shape),
                  _const_spec(w_cat.shape)],
        out_specs=[pl.BlockSpec((tm, w), lambda i: (i, 0)) for w in widths],
        out_shape=[jax.ShapeDtypeStruct((t, w), d) for w, d in zip(widths, dtypes)],
        scratch_shapes=[pltpu.VMEM((tm, A_WIDTH), BF16),
                        pltpu.VMEM((n_blk + 1, N_KEY_WINDOWS * 2 * WINDOW, LANES), BF16),
                        pltpu.VMEM((n_blk + 1, LANES, 2 * WINDOW), BF16),
                        pltpu.VMEM((CONV_HALO, 2 * M_WIDTH), F32)],
        compiler_params=pltpu.CompilerParams(dimension_semantics=("arbitrary",),
                                             vmem_limit_bytes=VMEM_LIMIT),
        name="in_proj",
    )(sinks, x2, g, pos, inv_freq, conv_qk, w_cat)


def _mlstm_mix_kernel(qk_ref, vo_ref, gate_ref, bias_ref, ngb_ref, ha_ref, x_ref,
                      wo_ref, gpost_ref, gpre_ref, wq_ref, h1_ref, xq_ref,
                      ct_ref, n_ref, m_ref, hm_ref, *, tiles_per_seq):
    L = M_CHUNK
    D = M_HEAD_DIM
    tm = qk_ref.shape[0]
    step = pl.program_id(0)
    write_slot = step % 2
    read_slot = (step + 1) % 2

    @pl.when(step % tiles_per_seq == 0)
    def _():
        ct_ref[...] = jnp.zeros_like(ct_ref)
        n_ref[...] = jnp.zeros_like(n_ref)
        m_ref[...] = jnp.zeros_like(m_ref)

    @pl.when(step == 0)
    def _():
        hm_ref[1] = jnp.zeros(hm_ref.shape[1:], hm_ref.dtype)

    row = lax.broadcasted_iota(jnp.int32, (L, L), 0)
    col = lax.broadcasted_iota(jnp.int32, (L, L), 1)
    tril = jnp.where(col <= row, 1.0, 0.0).astype(F32)
    key_le_query = row <= col
    sub8 = lax.broadcasted_iota(jnp.int32, (SUBLANES, 1), 0)

    def gate_terms(c):
        gates = gate_ref[c * L:(c + 1) * L, :] + bias_ref[...]
        bcum = jnp.dot(tril, jax.nn.log_sigmoid(gates), precision=lax.Precision.HIGHEST,
                       preferred_element_type=F32)
        return gates, bcum, gates.T, bcum.T

    def hi_lo_rows(v):
        hi = v.astype(BF16).astype(F32)
        return jnp.where(sub8 == 0, hi, jnp.where(sub8 == 1, v - hi, 0.0)).astype(BF16)

    def load_qk(c, h):
        qb = qk_ref[c * L:(c + 1) * L, h * D:(h + 1) * D]
        kb = qk_ref[c * L:(c + 1) * L, M_WIDTH + h * D:M_WIDTH + (h + 1) * D]
        return qb, kb, _dot_nt(kb, qb)

    def head_chain(h, all_terms):
        n_chunks = len(all_terms)
        nxt = load_qk(0, h)
        yield
        for c, terms in enumerate(all_terms):
            qb, kb, st = nxt
            inter_t = _dot_nt(ct_ref[h].astype(BF16), qb)
            qn2 = _dot_nt(hi_lo_rows(n_ref[h:h + 1, :]), qb)
            yield
            finish = chunk_body(c, h, terms, qb, kb, st, inter_t, qn2)
            next(finish)
            if c + 1 < n_chunks:
                nxt = load_qk(c + 1, h)
            yield
            next(finish, None)
            yield

    def chunk_body(c, h, terms, qb, kb, st, inter_t, qn2):
        gates, bcum, gates_t, bcum_t = terms
        c0 = h * D
        sr = h
        b_row = bcum_t[M_HEADS + h:M_HEADS + h + 1, :]
        i_row = gates_t[h:h + 1, :]
        c_col = gates[:, h:h + 1] - bcum[:, M_HEADS + h:M_HEADS + h + 1]
        m_prev = m_ref[sr:sr + 1, 0:1]
        g_row = b_row + m_prev
        dm_t = jnp.where(key_le_query, c_col + b_row, -jnp.inf)
        mj = jnp.maximum(g_row, jnp.max(dm_t, axis=0, keepdims=True))
        s_t = st * jnp.exp(dm_t - mj)
        vt = vo_ref[c * L:(c + 1) * L, c0:c0 + D].astype(F32).T
        intra_t = _dot(vt.astype(BF16), s_t.astype(BF16))
        b_last = b_row[:, L - 1:L]
        w_row = b_last - b_row + i_row
        m_new = jnp.maximum(b_last + m_prev, jnp.max(w_row, axis=1, keepdims=True))
        decay = jnp.exp(b_last + m_prev - m_new)
        wi_row = jnp.exp(w_row - m_new)
        ct_new = _dot((vt * wi_row).astype(BF16), kb)
        n_new2 = _dot(hi_lo_rows(wi_row), kb)
        yield
        wg = jnp.exp(g_row - mj)
        num_t = wg * inter_t + intra_t
        den = wg * (qn2[0:1, :] + qn2[1:2, :]) + jnp.sum(s_t, axis=0, keepdims=True)
        hh_t = num_t * (1.0 / jnp.maximum(jnp.abs(den), jnp.exp(-mj)))
        rs = lax.rsqrt(jnp.mean(hh_t * hh_t, axis=0, keepdims=True) + EPS)
        hn_t = hh_t * rs * ngb_ref[c0:c0 + D, :]
        og = vo_ref[c * L:(c + 1) * L, M_WIDTH + c0:M_WIDTH + c0 + D]
        sig = jnp.tanh(og * 0.5) * 0.5 + 0.5
        hm_ref[write_slot, c * L:(c + 1) * L, c0:c0 + D] = sig * hn_t.T.astype(BF16)
        ct_ref[sr] = decay * ct_ref[sr] + ct_new
        n_ref[sr:sr + 1, :] = decay * n_ref[sr:sr + 1, :] + n_new2[0:1, :] + n_new2[1:2, :]
        m_ref[sr:sr + 1, :] = jnp.broadcast_to(m_new, (1, LANES))

    def mix_chain():
        n_col = 2 * LANES
        parts = []
        for c0 in range(0, D_MODEL, n_col):
            part = _dot(hm_ref[read_slot], wo_ref[0:M_WIDTH, c0:c0 + n_col])
            yield
            parts.append(part + _dot(ha_ref[...], wo_ref[M_WIDTH:, c0:c0 + n_col]))
            yield
        h1 = x_ref[...] + _rms(jnp.concatenate(parts, axis=1), gpost_ref[...])
        h1_ref[...] = h1
        u = _rms(h1, gpre_ref[...]).astype(BF16)
        yield
        for c0 in range(0, D_MODEL, n_col):
            xq_ref[:, c0:c0 + n_col] = (_dot(u, wq_ref[:, c0:c0 + n_col])
                                        * (X_HEAD_DIM ** -0.5 * LOG2E)).astype(xq_ref.dtype)
            yield

    all_terms = [gate_terms(c) for c in range(tm // L)]
    chains = [head_chain(h, all_terms) for h in range(M_HEADS)] + [mix_chain()]
    _run_staggered(chains, start_per_round=len(chains))


def _mlstm_mix(qk, vo, gates, gate_bias, norm_g_lanes, ha, x2, w_out, g_post, g_pre, w_xq,
               tm, seq):
    t = x2.shape[0]
    n_tiles = t // tm
    cur = lambda s: (jnp.minimum(s, n_tiles - 1), 0)
    prev = lambda s: (jnp.maximum(s - 1, 0), 0)
    return pl.pallas_call(
        functools.partial(_mlstm_mix_kernel, tiles_per_seq=seq // tm),
        grid=(n_tiles + 1,),
        in_specs=[pl.BlockSpec((tm, 2 * M_WIDTH), cur),
                  pl.BlockSpec((tm, 2 * M_WIDTH), cur),
                  pl.BlockSpec((tm, LANES), cur),
                  _const_spec((1, LANES)),
                  _const_spec((M_WIDTH, LANES)),
                  pl.BlockSpec((tm, A_WIDTH), prev),
                  pl.BlockSpec((tm, D_MODEL), prev),
                  _const_spec(w_out.shape),
                  _const_spec((1, D_MODEL)), _const_spec((1, D_MODEL)),
                  _const_spec(w_xq.shape)],
        out_specs=[pl.BlockSpec((tm, D_MODEL), prev), pl.BlockSpec((tm, D_MODEL), prev)],
        out_shape=[jax.ShapeDtypeStruct((t, D_MODEL), F32),
                   jax.ShapeDtypeStruct((t, D_MODEL), BF16)],
        scratch_shapes=[pltpu.VMEM((M_HEADS, M_HEAD_DIM, M_HEAD_DIM), F32),
                        pltpu.VMEM((M_HEADS, M_HEAD_DIM), F32),
                        pltpu.VMEM((M_HEADS, LANES), F32),
                        pltpu.VMEM((2, tm, M_WIDTH), BF16)],
        compiler_params=pltpu.CompilerParams(dimension_semantics=("arbitrary",),
                                             vmem_limit_bytes=VMEM_LIMIT),
        name="mlstm_mix",
    )(qk, vo, gates, gate_bias, norm_g_lanes, ha, x2, w_out, g_post, g_pre, w_xq)


def _mem_kv_kernel(mem_ref, g_ref, w_ref, kv_ref):
    u = _rms(mem_ref[...], g_ref[...]).astype(BF16)
    kv_ref[...] = _dot(u, w_ref[...]).astype(kv_ref.dtype)


def _mem_kv(mem2, g, w_xkv, tm):
    t = mem2.shape[0]
    assert t % tm == 0
    return pl.pallas_call(
        _mem_kv_kernel,
        grid=(t // tm,),
        in_specs=[pl.BlockSpec((tm, D_MODEL), lambda i: (i, 0)),
                  _const_spec((1, D_MODEL)),
                  _const_spec(w_xkv.shape)],
        out_specs=pl.BlockSpec((tm, 2 * D_MODEL), lambda i: (i, 0)),
        out_shape=jax.ShapeDtypeStruct((t, 2 * D_MODEL), BF16),
        compiler_params=pltpu.CompilerParams(dimension_semantics=("arbitrary",),
                                             vmem_limit_bytes=VMEM_LIMIT),
        name="mem_kv",
    )(mem2, g, w_xkv)


def _xattn_ffn_kernel(h1_ref, xq_ref, kv_ref, wo_ref, gxpost_ref, gfpre_ref, wgu_ref, wd_ref,
                      gfpost_ref, out_ref, o_ref, *, ff_chunk):
    d_ff = wd_ref.shape[0]
    chunks = [(j0, min(j0 + ff_chunk, d_ff)) for j0 in range(0, d_ff, ff_chunk)]

    def rows_stages(r0, r1):
        heads = [h * X_HEAD_DIM for h in range(X_HEADS)]
        scores = [_dot_nt(xq_ref[r0:r1, c0:c0 + X_HEAD_DIM], kv_ref[:, c0:c0 + X_HEAD_DIM])
                  for c0 in heads]
        yield
        for c0, s in zip(heads, scores):
            e = jnp.exp2(s - jnp.max(s, axis=1, keepdims=True))
            inv = 1.0 / jnp.sum(e, axis=1, keepdims=True)
            o = _dot(e.astype(BF16), kv_ref[:, D_MODEL + c0:D_MODEL + c0 + X_HEAD_DIM])
            o_ref[r0:r1, c0:c0 + X_HEAD_DIM] = (o * inv).astype(BF16)
        yield
        c = _dot(o_ref[r0:r1, :], wo_ref[...])
        yield
        h2 = h1_ref[r0:r1, :] + _rms(c, gxpost_ref[...])
        u = _rms(h2, gfpre_ref[...]).astype(BF16)
        f = None
        for j0, j1 in chunks:
            gate = _dot(u, wgu_ref[:, j0:j1])
            up = _dot(u, wgu_ref[:, d_ff + j0:d_ff + j1])
            yield
            act = (gate * jax.nn.sigmoid(gate) * up).astype(BF16)
            part = _dot(act, wd_ref[j0:j1, :])
            f = part if f is None else f + part
        yield
        out_ref[r0:r1, :] = h2 + _rms(f, gfpost_ref[...])

    tm = h1_ref.shape[0]
    step = tm // ROW_GROUPS
    _run_staggered([rows_stages(r0, r0 + step) for r0 in range(0, tm, step)],
                   start_per_round=ROW_GROUPS)


def _xattn_ffn(h1, xq, kv, w_xo, g_xpost, g_fpre, w_gu, w_d, g_fpost, tm, seq, mem_len, ff_chunk):
    t = h1.shape[0]
    row = lambda i: (i, 0)
    per_b = seq // tm
    return pl.pallas_call(
        functools.partial(_xattn_ffn_kernel, ff_chunk=ff_chunk),
        grid=(t // tm,),
        in_specs=[pl.BlockSpec((tm, D_MODEL), row),
                  pl.BlockSpec((tm, D_MODEL), row),
                  pl.BlockSpec((mem_len, 2 * D_MODEL), lambda i: (i // per_b, 0)),
                  _const_spec(w_xo.shape),
                  _const_spec((1, D_MODEL)), _const_spec((1, D_MODEL)),
                  _const_spec(w_gu.shape), _const_spec(w_d.shape),
                  _const_spec((1, D_MODEL))],
        out_specs=pl.BlockSpec((tm, D_MODEL), row),
        out_shape=jax.ShapeDtypeStruct((t, D_MODEL), F32),
        scratch_shapes=[pltpu.VMEM((tm, D_MODEL), BF16)],
        compiler_params=pltpu.CompilerParams(dimension_semantics=("arbitrary",),
                                             vmem_limit_bytes=VMEM_LIMIT),
        name="xattn_ffn",
    )(h1, xq, kv, w_xo, g_xpost, g_fpre, w_gu, w_d, g_fpost)


def _layer(h, mem2, pos2, inv_freq, p, batch, seq, mem_len):
    w_in = p["w_in"].astype(BF16)
    o_i = 4 * M_WIDTH
    o_qa = o_i + 2 * M_HEADS
    o_ka = o_qa + A_WIDTH
    w_gate = jnp.pad(w_in[:, o_i:o_qa], ((0, 0), (0, LANES - 2 * M_HEADS)))

    def pair_interleave(w):
        w5 = w.reshape(w.shape[0], -1, 2, 2, A_HEAD_DIM // 2)
        return jnp.swapaxes(w5, 2, 3).reshape(w.shape)

    w_cat = jnp.concatenate([w_in[:, :o_i], pair_interleave(w_in[:, o_qa:o_ka]),
                             pair_interleave(w_in[:, o_ka:o_ka + A_KV_WIDTH]),
                             w_in[:, o_ka + A_KV_WIDTH:], w_gate], axis=1)
    gate_bias = jnp.pad(jnp.concatenate([p["i_bias"], p["f_bias"]]),
                        (0, LANES - 2 * M_HEADS)).reshape(1, LANES)
    row = lambda v: v.reshape(1, -1)

    qk, vo, ha, gates = _in_proj(p["attn_sinks"], h, row(p["mix_pre_g"]), pos2, inv_freq,
                                 p["conv_qk"], w_cat, tm=512, seq=seq)
    norm_g_lanes = jnp.broadcast_to(p["mlstm_norm_g"][:, None], (M_WIDTH, LANES))
    h1, xq = _mlstm_mix(qk, vo, gates, gate_bias, norm_g_lanes, ha, h,
                        p["w_out"].astype(BF16), row(p["mix_post_g"]),
                        row(p["xattn_pre_g"]), p["w_xq"].astype(BF16), tm=512, seq=seq)
    kv = _mem_kv(mem2, row(p["mem_norm_g"]), p["w_xkv"].astype(BF16), tm=1024)
    return _xattn_ffn(h1, xq, kv, p["w_xo"].astype(BF16), row(p["xattn_post_g"]),
                      row(p["ffn_pre_g"]), p["w_gate_up"].astype(BF16), p["w_down"].astype(BF16),
                      row(p["ffn_post_g"]), tm=512, seq=seq, mem_len=mem_len, ff_chunk=1536)


def kernel(x, mem, positions, mix_pre_g, mix_post_g, w_in, conv_qk, f_bias, i_bias, mlstm_norm_g,
           attn_sinks, w_out, xattn_pre_g, xattn_post_g, mem_norm_g, w_xq, w_xkv, w_xo,
           ffn_pre_g, ffn_post_g, w_gate_up, w_down):
    batch, seq, _ = x.shape
    mem_len = mem.shape[1]
    depth = w_in.shape[0]
    params = dict(mix_pre_g=mix_pre_g, mix_post_g=mix_post_g, w_in=w_in, conv_qk=conv_qk,
                  f_bias=f_bias, i_bias=i_bias, mlstm_norm_g=mlstm_norm_g, attn_sinks=attn_sinks,
                  w_out=w_out, xattn_pre_g=xattn_pre_g, xattn_post_g=xattn_post_g,
                  mem_norm_g=mem_norm_g, w_xq=w_xq, w_xkv=w_xkv, w_xo=w_xo, ffn_pre_g=ffn_pre_g,
                  ffn_post_g=ffn_post_g, w_gate_up=w_gate_up, w_down=w_down)
    inv = ROPE_THETA ** (-jnp.arange(0, A_HEAD_DIM, 2, dtype=F32) / A_HEAD_DIM)
    inv_freq = jnp.tile(inv, LANES // inv.shape[0]).reshape(1, LANES)
    h = x.reshape(batch * seq, D_MODEL)
    mem2 = mem.reshape(batch * mem_len, D_MODEL)
    pos2 = positions.reshape(batch * seq, 1)
    for l in range(depth):
        h = _layer(h, mem2, pos2, inv_freq, {k: v[l] for k, v in params.items()},
                   batch, seq, mem_len)
    return h.reshape(batch, seq, D_MODEL)
```

```python
import functools

import jax
import jax.numpy as jnp
from jax import lax
from jax.experimental import pallas as pl
from jax.experimental.pallas import tpu as pltpu

D_MODEL = 1024
EPS = 1e-6
M_HEADS = 4
M_HEAD_DIM = 256
M_WIDTH = M_HEADS * M_HEAD_DIM
M_CONV = 4
M_CHUNK = 128
A_Q_HEADS = 16
A_KV_HEADS = 2
A_HEAD_DIM = 64
A_WIDTH = A_Q_HEADS * A_HEAD_DIM
A_KV_WIDTH = A_KV_HEADS * A_HEAD_DIM
WINDOW = 128
ROPE_THETA = 10000.0
LOG2E = 1.4426950408889634
X_HEADS = 4
X_HEAD_DIM = D_MODEL // X_HEADS
LANES = 128
SUBLANES = 8
N_KEY_WINDOWS = 2 * A_KV_HEADS
CONV_HALO = SUBLANES
ROW_GROUPS = 2
WEIGHT_STAGE_CHUNKS = 8
SCORE_LOOKAHEAD = 3
EARLY_STREAM_CHUNKS = 2
VMEM_LIMIT = 56 * 1024 * 1024

F32 = jnp.float32
BF16 = jnp.bfloat16


def _rms(x, g):
    return x * lax.rsqrt(jnp.mean(x * x, axis=-1, keepdims=True) + EPS) * g


def _dot(a, b):
    return jnp.dot(a, b, preferred_element_type=F32)


def _dot_nt(a, b):
    return lax.dot_general(a, b, (((1,), (1,)), ((), ())), preferred_element_type=F32)


def _run_staggered(chains, start_per_round=1):
    pending = list(chains)
    running = []
    while pending or running:
        for _ in range(min(start_per_round, len(pending))):
            running.append(pending.pop(0))
        for gen in list(running):
            if next(gen, "done") == "done":
                running.remove(gen)


def _const_spec(shape):
    nd = len(shape)
    return pl.BlockSpec(shape, lambda *_: (0,) * nd, pipeline_mode=pl.Buffered(1))


def _in_proj_kernel(sink_ref, x_ref, g_ref, pos_ref, inv_ref, conv_ref, w_ref, qk_ref, vo_ref,
                    ha_ref, gate_ref, q_scr, k_ref, vt_ref, halo_ref, *, tiles_per_seq):
    W = WINDOW
    tm = x_ref.shape[0]
    n_blk = tm // W
    half = A_HEAD_DIM // 2
    first_tile = pl.program_id(0) % tiles_per_seq == 0

    @pl.when(first_tile)
    def _():
        for s in range(N_KEY_WINDOWS):
            k_ref[0, s * 2 * W:s * 2 * W + W, :] = jnp.zeros((W, LANES), BF16)
        vt_ref[0, :, 0:W] = jnp.zeros((LANES, W), BF16)
        halo_ref[...] = jnp.zeros_like(halo_ref)

    @pl.when(jnp.logical_not(first_tile))
    def _():
        for s in range(N_KEY_WINDOWS):
            k_ref[0, s * 2 * W:s * 2 * W + W, :] = k_ref[n_blk, s * 2 * W:s * 2 * W + W, :]
        vt_ref[0, :, 0:W] = vt_ref[n_blk, :, 0:W]

    u = _rms(x_ref[...], g_ref[...]).astype(BF16)

    stream_chunks = [(ref, c0, base + c0)
                     for c0 in range(0, 2 * M_WIDTH, 2 * LANES)
                     for ref, base in ((qk_ref, 0), (vo_ref, 2 * M_WIDTH))]

    def project_chunk():
        ref, c0, wc = stream_chunks.pop(0)
        t = _dot(u, w_ref[:, wc:wc + 2 * LANES])
        if ref is vo_ref:
            ref[:, c0:c0 + 2 * LANES] = t.astype(ref.dtype)
            return
        assert M_CONV == 4
        ext = jnp.concatenate([halo_ref[:, c0:c0 + 2 * LANES], t], axis=0)
        prev1 = pltpu.roll(ext, 1, 0)
        taps = [conv_ref[j:j + 1, c0:c0 + 2 * LANES] for j in range(M_CONV)]
        older = ext * taps[1] + prev1 * taps[0]
        acc = (t * taps[3] + prev1[CONV_HALO:, :] * taps[2]
               + pltpu.roll(older, 2, 0)[CONV_HALO:, :])
        halo_ref[:, c0:c0 + 2 * LANES] = t[tm - CONV_HALO:, :]
        scale = M_HEAD_DIM ** -0.5 if c0 >= M_WIDTH else 1.0
        half_x = acc * (0.5 * scale)
        arg = half_x if scale == 1.0 else half_x * (1.0 / scale)
        ref[:, c0:c0 + 2 * LANES] = (half_x * jnp.tanh(arg) + half_x).astype(ref.dtype)

    for _ in range(EARLY_STREAM_CHUNKS):
        project_chunk()
    off = 4 * M_WIDTH

    n_freq = A_HEAD_DIM // 2
    n_seg = LANES // n_freq
    rows = tm // n_seg
    seg = lax.broadcasted_iota(jnp.int32, (rows, LANES), 1) // n_freq
    pos_f = pos_ref[...].astype(F32)
    pos_c = jnp.broadcast_to(pos_f[0:rows], (rows, LANES))
    for s in range(1, n_seg):
        pos_c = jnp.where(seg == s, pos_f[s * rows:(s + 1) * rows], pos_c)
    ang_c = pos_c * inv_ref[...]

    def spread(table_c):
        rolled = [table_c] + [pltpu.roll(table_c, n_freq * k, 1) for k in range(1, n_seg)]
        chunks = []
        for s in range(n_seg):
            full = rolled[(0 - s) % n_seg]
            for p in range(1, n_seg):
                full = jnp.where(seg == p, rolled[(p - s) % n_seg], full)
            chunks.append(full)
        return jnp.concatenate(chunks, axis=0)

    lane = lax.broadcasted_iota(jnp.int32, (tm, LANES), 1)
    cos_t = spread(jnp.cos(ang_c))
    sin_t = spread(jnp.sin(ang_c))
    sin_t = jnp.where(lane < LANES // 2, -sin_t, sin_t)

    def rope(t, c, s):
        return t * c + pltpu.roll(t, LANES // 2, 1) * s

    scale = A_HEAD_DIM ** -0.5 * LOG2E
    cos_q = cos_t * scale
    sin_q = sin_t * scale
    o_kv = off + A_WIDTH
    kv = _dot(u, w_ref[:, o_kv:o_kv + 2 * LANES])
    k_rot = rope(kv[:, 0:LANES], cos_t, sin_t)
    slot_a = (lane % A_HEAD_DIM) < half
    variants = (jnp.where(slot_a, k_rot, 0.0),
                jnp.where(slot_a, 0.0, pltpu.roll(k_rot, half, 1)),
                jnp.where(slot_a, pltpu.roll(k_rot, LANES - half, 1), 0.0),
                jnp.where(slot_a, 0.0, k_rot))
    for j in range(n_blk):
        for s, var in enumerate(variants):
            blk = var[j * W:(j + 1) * W, :].astype(BF16)
            k_ref[j, s * 2 * W + W:(s + 1) * 2 * W, :] = blk
            k_ref[j + 1, s * 2 * W:s * 2 * W + W, :] = blk
        vt = kv[j * W:(j + 1) * W, LANES:2 * LANES].T.astype(BF16)
        vt_ref[j, :, W:2 * W] = vt
        vt_ref[j + 1, :, 0:W] = vt

    for j in range(A_WIDTH // (2 * LANES)):
        t = _dot(u, w_ref[:, off:off + 2 * LANES])
        for i in range(2):
            c0 = (2 * j + i) * LANES
            q_scr[:, c0:c0 + LANES] = rope(t[:, i * LANES:(i + 1) * LANES],
                                           cos_q, sin_q).astype(q_scr.dtype)
        off += 2 * LANES
    gate_ref[...] = _dot(u, w_ref[:, o_kv + 2 * LANES:o_kv + 3 * LANES])

    kpos = lax.broadcasted_iota(jnp.int32, (2 * W, W), 0)
    qpos = lax.broadcasted_iota(jnp.int32, (2 * W, W), 1)
    diff = W + qpos - kpos
    band = (diff >= 0) & (diff < W)
    bias_rest = jnp.where(band, 0.0, -jnp.inf).astype(F32)
    bias_first = jnp.where(band & (jnp.logical_not(first_tile) | (kpos >= W)),
                           0.0, -jnp.inf).astype(F32)

    n_pairs = A_Q_HEADS // 2
    pairs_per_group = n_pairs // A_KV_HEADS
    items = [(j, pair) for j in range(n_blk) for pair in range(n_pairs)]

    def scores(j, pair):
        g = pair // pairs_per_group
        q = q_scr[j * W:(j + 1) * W, pair * LANES:(pair + 1) * LANES]
        return _dot_nt(k_ref[j, g * 4 * W:(g + 1) * 4 * W, :], q)

    n_late = len(stream_chunks)
    pending = [scores(*it) for it in items[:SCORE_LOOKAHEAD]]
    for idx, (j, pair) in enumerate(items):
        g = pair // pairs_per_group
        if idx + SCORE_LOOKAHEAD < len(items):
            pending.append(scores(*items[idx + SCORE_LOOKAHEAD]))
        st_pair = pending.pop(0)
        bias = bias_first if j == 0 else bias_rest
        pes, inv_dens = [], []
        for slot in range(2):
            sink = sink_ref[2 * pair + slot] * LOG2E
            st = st_pair[slot * 2 * W:(slot + 1) * 2 * W, :] + bias
            mx = jnp.maximum(jnp.max(st, axis=0, keepdims=True), sink)
            pe = jnp.exp2(st - mx)
            den = jnp.sum(pe, axis=0, keepdims=True) + jnp.exp2(sink - mx)
            pes.append(pe.astype(BF16))
            inv_dens.append(1.0 / den)
        ot = _dot(vt_ref[j, g * A_HEAD_DIM:(g + 1) * A_HEAD_DIM, :],
                  jnp.concatenate(pes, axis=1))
        o_pair = jnp.concatenate([ot[:, slot * W:(slot + 1) * W] * inv_dens[slot]
                                  for slot in range(2)], axis=0)
        ha_ref[j * W:(j + 1) * W, pair * LANES:(pair + 1) * LANES] = (
            o_pair.T.astype(ha_ref.dtype))
        if (idx + 1) * n_late // len(items) > idx * n_late // len(items):
            project_chunk()
    assert not stream_chunks


def _in_proj(sinks, x2, g, pos, inv_freq, conv_qk, w_cat, tm, seq):
    t = x2.shape[0]
    widths = (2 * M_WIDTH, 2 * M_WIDTH, A_WIDTH, LANES)
    dtypes = (BF16, BF16, BF16, F32)
    n_blk = tm // WINDOW
    return pl.pallas_call(
        functools.partial(_in_proj_kernel, tiles_per_seq=seq // tm),
        grid=(t // tm,),
        in_specs=[pl.BlockSpec(memory_space=pltpu.SMEM),
                  pl.BlockSpec((tm, D_MODEL), lambda i: (i, 0)),
                  _const_spec((1, D_MODEL)),
                  pl.BlockSpec((tm, 1), lambda i: (i, 0)),
                  _const_spec((1, LANES)),
                  _const_spec(conv_qk.shape),
                  _const_spec(w_cat.shape)],
        out_specs=[pl.BlockSpec((tm, w), lambda i: (i, 0)) for w in widths],
        out_shape=[jax.ShapeDtypeStruct((t, w), d) for w, d in zip(widths, dtypes)],
        scratch_shapes=[pltpu.VMEM((tm, A_WIDTH), BF16),
                        pltpu.VMEM((n_blk + 1, N_KEY_WINDOWS * 2 * WINDOW, LANES), BF16),
                        pltpu.VMEM((n_blk + 1, LANES, 2 * WINDOW), BF16),
                        pltpu.VMEM((CONV_HALO, 2 * M_WIDTH), F32)],
        compiler_params=pltpu.CompilerParams(dimension_semantics=("arbitrary",),
                                             vmem_limit_bytes=VMEM_LIMIT),
        name="in_proj",
    )(sinks, x2, g, pos, inv_freq, conv_qk, w_cat)


def _mlstm_mix_kernel(qk_ref, vo_ref, gate_ref, bias_ref, ngb_ref, ha_ref, x_ref,
                      wo_ref, gpost_ref, gpre_ref, wq_ref, h1_ref, xq_ref,
                      ct_ref, n_ref, m_ref, hm_ref, *, tiles_per_seq):
    L = M_CHUNK
    D = M_HEAD_DIM
    tm = qk_ref.shape[0]
    step = pl.program_id(0)
    write_slot = step % 2
    read_slot = (step + 1) % 2

    @pl.when(step % tiles_per_seq == 0)
    def _():
        ct_ref[...] = jnp.zeros_like(ct_ref)
        n_ref[...] = jnp.zeros_like(n_ref)
        m_ref[...] = jnp.zeros_like(m_ref)

    @pl.when(step == 0)
    def _():
        hm_ref[1] = jnp.zeros(hm_ref.shape[1:], hm_ref.dtype)

    row = lax.broadcasted_iota(jnp.int32, (L, L), 0)
    col = lax.broadcasted_iota(jnp.int32, (L, L), 1)
    tril = jnp.where(col <= row, 1.0, 0.0).astype(F32)
    key_le_query = row <= col
    sub8 = lax.broadcasted_iota(jnp.int32, (SUBLANES, 1), 0)

    def gate_terms(c):
        gates = gate_ref[c * L:(c + 1) * L, :] + bias_ref[...]
        bcum = jnp.dot(tril, jax.nn.log_sigmoid(gates), precision=lax.Precision.HIGHEST,
                       preferred_element_type=F32)
        return gates, bcum, gates.T, bcum.T

    def hi_lo_rows(v):
        hi = v.astype(BF16).astype(F32)
        return jnp.where(sub8 == 0, hi, jnp.where(sub8 == 1, v - hi, 0.0)).astype(BF16)

    def load_qk(c, h):
        qb = qk_ref[c * L:(c + 1) * L, h * D:(h + 1) * D]
        kb = qk_ref[c * L:(c + 1) * L, M_WIDTH + h * D:M_WIDTH + (h + 1) * D]
        return qb, kb, _dot_nt(kb, qb)

    def head_chain(h, all_terms):
        n_chunks = len(all_terms)
        nxt = load_qk(0, h)
        yield
        for c, terms in enumerate(all_terms):
            qb, kb, st = nxt
            inter_t = _dot_nt(ct_ref[h].astype(BF16), qb)
            qn2 = _dot_nt(hi_lo_rows(n_ref[h:h + 1, :]), qb)
            yield
            finish = chunk_body(c, h, terms, qb, kb, st, inter_t, qn2)
            next(finish)
            if c + 1 < n_chunks:
                nxt = load_qk(c + 1, h)
            yield
            next(finish, None)
            yield

    def chunk_body(c, h, terms, qb, kb, st, inter_t, qn2):
        gates, bcum, gates_t, bcum_t = terms
        c0 = h * D
        sr = h
        b_row = bcum_t[M_HEADS + h:M_HEADS + h + 1, :]
        i_row = gates_t[h:h + 1, :]
        c_col = gates[:, h:h + 1] - bcum[:, M_HEADS + h:M_HEADS + h + 1]
        m_prev = m_ref[sr:sr + 1, 0:1]
        g_row = b_row + m_prev
        dm_t = jnp.where(key_le_query, c_col + b_row, -jnp.inf)
        mj = jnp.maximum(g_row, jnp.max(dm_t, axis=0, keepdims=True))
        s_t = st * jnp.exp(dm_t - mj)
        vt = vo_ref[c * L:(c + 1) * L, c0:c0 + D].astype(F32).T
        intra_t = _dot(vt.astype(BF16), s_t.astype(BF16))
        b_last = b_row[:, L - 1:L]
        w_row = b_last - b_row + i_row
        m_new = jnp.maximum(b_last + m_prev, jnp.max(w_row, axis=1, keepdims=True))
        decay = jnp.exp(b_last + m_prev - m_new)
        wi_row = jnp.exp(w_row - m_new)
        ct_new = _dot((vt * wi_row).astype(BF16), kb)
        n_new2 = _dot(hi_lo_rows(wi_row), kb)
        yield
        wg = jnp.exp(g_row - mj)
        num_t = wg * inter_t + intra_t
        den = wg * (qn2[0:1, :] + qn2[1:2, :]) + jnp.sum(s_t, axis=0, keepdims=True)
        hh_t = num_t * (1.0 / jnp.maximum(jnp.abs(den), jnp.exp(-mj)))
        rs = lax.rsqrt(jnp.mean(hh_t * hh_t, axis=0, keepdims=True) + EPS)
        hn_t = hh_t * rs * ngb_ref[c0:c0 + D, :]
        og = vo_ref[c * L:(c + 1) * L, M_WIDTH + c0:M_WIDTH + c0 + D]
        sig = jnp.tanh(og * 0.5) * 0.5 + 0.5
        hm_ref[write_slot, c * L:(c + 1) * L, c0:c0 + D] = sig * hn_t.T.astype(BF16)
        ct_ref[sr] = decay * ct_ref[sr] + ct_new
        n_ref[sr:sr + 1, :] = decay * n_ref[sr:sr + 1, :] + n_new2[0:1, :] + n_new2[1:2, :]
        m_ref[sr:sr + 1, :] = jnp.broadcast_to(m_new, (1, LANES))

    def mix_chain():
        n_col = 2 * LANES
        parts = []
        for c0 in range(0, D_MODEL, n_col):
            part = _dot(hm_ref[read_slot], wo_ref[0:M_WIDTH, c0:c0 + n_col])
            yield
            parts.append(part + _dot(ha_ref[...], wo_ref[M_WIDTH:, c0:c0 + n_col]))
            yield
        h1 = x_ref[...] + _rms(jnp.concatenate(parts, axis=1), gpost_ref[...])
        h1_ref[...] = h1
        u = _rms(h1, gpre_ref[...]).astype(BF16)
        yield
        for c0 in range(0, D_MODEL, n_col):
            xq_ref[:, c0:c0 + n_col] = (_dot(u, wq_ref[:, c0:c0 + n_col])
                                        * (X_HEAD_DIM ** -0.5 * LOG2E)).astype(xq_ref.dtype)
            yield

    all_terms = [gate_terms(c) for c in range(tm // L)]
    chains = [head_chain(h, all_terms) for h in range(M_HEADS)] + [mix_chain()]
    _run_staggered(chains, start_per_round=len(chains))


def _mlstm_mix(qk, vo, gates, gate_bias, norm_g_lanes, ha, x2, w_out, g_post, g_pre, w_xq,
               tm, seq):
    t = x2.shape[0]
    n_tiles = t // tm
    cur = lambda s: (jnp.minimum(s, n_tiles - 1), 0)
    prev = lambda s: (jnp.maximum(s - 1, 0), 0)
    return pl.pallas_call(
        functools.partial(_mlstm_mix_kernel, tiles_per_seq=seq // tm),
        grid=(n_tiles + 1,),
        in_specs=[pl.BlockSpec((tm, 2 * M_WIDTH), cur),
                  pl.BlockSpec((tm, 2 * M_WIDTH), cur),
                  pl.BlockSpec((tm, LANES), cur),
                  _const_spec((1, LANES)),
                  _const_spec((M_WIDTH, LANES)),
                  pl.BlockSpec((tm, A_WIDTH), prev),
                  pl.BlockSpec((tm, D_MODEL), prev),
                  _const_spec(w_out.shape),
                  _const_spec((1, D_MODEL)), _const_spec((1, D_MODEL)),
                  _const_spec(w_xq.shape)],
        out_specs=[pl.BlockSpec((tm, D_MODEL), prev), pl.BlockSpec((tm, D_MODEL), prev)],
        out_shape=[jax.ShapeDtypeStruct((t, D_MODEL), F32),
                   jax.ShapeDtypeStruct((t, D_MODEL), BF16)],
        scratch_shapes=[pltpu.VMEM((M_HEADS, M_HEAD_DIM, M_HEAD_DIM), F32),
                        pltpu.VMEM((M_HEADS, M_HEAD_DIM), F32),
                        pltpu.VMEM((M_HEADS, LANES), F32),
                        pltpu.VMEM((2, tm, M_WIDTH), BF16)],
        compiler_params=pltpu.CompilerParams(dimension_semantics=("arbitrary",),
                                             vmem_limit_bytes=VMEM_LIMIT),
        name="mlstm_mix",
    )(qk, vo, gates, gate_bias, norm_g_lanes, ha, x2, w_out, g_post, g_pre, w_xq)


def _mem_kv_kernel(mem_ref, g_ref, w_ref, kv_ref):
    u = _rms(mem_ref[...], g_ref[...]).astype(BF16)
    kv_ref[...] = _dot(u, w_ref[...]).astype(kv_ref.dtype)


def _mem_kv(mem2, g, w_xkv, tm):
    t = mem2.shape[0]
    assert t % tm == 0
    return pl.pallas_call(
        _mem_kv_kernel,
        grid=(t // tm,),
        in_specs=[pl.BlockSpec((tm, D_MODEL), lambda i: (i, 0)),
                  _const_spec((1, D_MODEL)),
                  _const_spec(w_xkv.shape)],
        out_specs=pl.BlockSpec((tm, 2 * D_MODEL), lambda i: (i, 0)),
        out_shape=jax.ShapeDtypeStruct((t, 2 * D_MODEL), BF16),
        compiler_params=pltpu.CompilerParams(dimension_semantics=("arbitrary",),
                                             vmem_limit_bytes=VMEM_LIMIT),
        name="mem_kv",
    )(mem2, g, w_xkv)


def _load_cast(src_hbm, dst_ref, stage_ref, sem_ref):
    rows = stage_ref.shape[1]
    n = src_hbm.shape[0] // rows

    def copy(c):
        return pltpu.make_async_copy(src_hbm.at[pl.ds(c * rows, rows)], stage_ref.at[c % 2],
                                     sem_ref.at[c % 2])

    copy(0).start()
    for c in range(n):
        if c + 1 < n:
            copy(c + 1).start()
        copy(c).wait()
        dst_ref[c * rows:(c + 1) * rows, :] = stage_ref[c % 2].astype(dst_ref.dtype)


def _xattn_ffn_kernel(h1_ref, xq_ref, kv_ref, wo_hbm, gxpost_ref, gfpre_ref, wgu_hbm, wd_hbm,
                      gfpost_ref, out_ref, o_ref, wo_ref, wgu_ref, wd_ref, stage_o, stage_gu,
                      stage_d, sem_o, sem_gu, sem_d, *, ff_chunk):
    @pl.when(pl.program_id(0) == 0)
    def _():
        _load_cast(wo_hbm, wo_ref, stage_o, sem_o)
        _load_cast(wgu_hbm, wgu_ref, stage_gu, sem_gu)
        _load_cast(wd_hbm, wd_ref, stage_d, sem_d)

    d_ff = wd_ref.shape[0]
    chunks = [(j0, min(j0 + ff_chunk, d_ff)) for j0 in range(0, d_ff, ff_chunk)]

    def rows_stages(r0, r1):
        heads = [h * X_HEAD_DIM for h in range(X_HEADS)]
        scores = [_dot_nt(xq_ref[r0:r1, c0:c0 + X_HEAD_DIM], kv_ref[:, c0:c0 + X_HEAD_DIM])
                  for c0 in heads]
        yield
        for c0, s in zip(heads, scores):
            e = jnp.exp2(s - jnp.max(s, axis=1, keepdims=True))
            inv = 1.0 / jnp.sum(e, axis=1, keepdims=True)
            o = _dot(e.astype(BF16), kv_ref[:, D_MODEL + c0:D_MODEL + c0 + X_HEAD_DIM])
            o_ref[r0:r1, c0:c0 + X_HEAD_DIM] = (o * inv).astype(BF16)
        yield
        c = _dot(o_ref[r0:r1, :], wo_ref[...])
        yield
        h2 = h1_ref[r0:r1, :] + _rms(c, gxpost_ref[...])
        u = _rms(h2, gfpre_ref[...]).astype(BF16)
        f = None
        for j0, j1 in chunks:
            gate = _dot(u, wgu_ref[:, j0:j1])
            up = _dot(u, wgu_ref[:, d_ff + j0:d_ff + j1])
            yield
            act = (gate * jax.nn.sigmoid(gate) * up).astype(BF16)
            part = _dot(act, wd_ref[j0:j1, :])
            f = part if f is None else f + part
        yield
        out_ref[r0:r1, :] = h2 + _rms(f, gfpost_ref[...])

    tm = h1_ref.shape[0]
    step = tm // ROW_GROUPS
    _run_staggered([rows_stages(r0, r0 + step) for r0 in range(0, tm, step)],
                   start_per_round=ROW_GROUPS)


def _xattn_ffn(h1, xq, kv, w_xo, g_xpost, g_fpre, w_gu, w_d, g_fpost, tm, seq, mem_len, ff_chunk):
    t = h1.shape[0]
    row = lambda i: (i, 0)
    per_b = seq // tm
    return pl.pallas_call(
        functools.partial(_xattn_ffn_kernel, ff_chunk=ff_chunk),
        grid=(t // tm,),
        in_specs=[pl.BlockSpec((tm, D_MODEL), row),
                  pl.BlockSpec((tm, D_MODEL), row),
                  pl.BlockSpec((mem_len, 2 * D_MODEL), lambda i: (i // per_b, 0)),
                  pl.BlockSpec(memory_space=pl.ANY),
                  _const_spec((1, D_MODEL)), _const_spec((1, D_MODEL)),
                  pl.BlockSpec(memory_space=pl.ANY), pl.BlockSpec(memory_space=pl.ANY),
                  _const_spec((1, D_MODEL))],
        out_specs=pl.BlockSpec((tm, D_MODEL), row),
        out_shape=jax.ShapeDtypeStruct((t, D_MODEL), F32),
        scratch_shapes=[pltpu.VMEM((tm, D_MODEL), BF16),
                        pltpu.VMEM(w_xo.shape, BF16), pltpu.VMEM(w_gu.shape, BF16),
                        pltpu.VMEM(w_d.shape, BF16),
                        pltpu.VMEM((2, w_xo.shape[0] // WEIGHT_STAGE_CHUNKS, w_xo.shape[1]), F32),
                        pltpu.VMEM((2, w_gu.shape[0] // (4 * WEIGHT_STAGE_CHUNKS), w_gu.shape[1]),
                                   F32),
                        pltpu.VMEM((2, w_d.shape[0] // (2 * WEIGHT_STAGE_CHUNKS), w_d.shape[1]),
                                   F32),
                        pltpu.SemaphoreType.DMA((2,)), pltpu.SemaphoreType.DMA((2,)),
                        pltpu.SemaphoreType.DMA((2,))],
        compiler_params=pltpu.CompilerParams(dimension_semantics=("arbitrary",),
                                             vmem_limit_bytes=VMEM_LIMIT),
        name="xattn_ffn",
    )(h1, xq, kv, w_xo, g_xpost, g_fpre, w_gu, w_d, g_fpost)


def _layer(h, mem2, pos2, inv_freq, p, batch, seq, mem_len):
    w_in = p["w_in"].astype(BF16)
    o_i = 4 * M_WIDTH
    o_qa = o_i + 2 * M_HEADS
    o_ka = o_qa + A_WIDTH
    w_gate = jnp.pad(w_in[:, o_i:o_qa], ((0, 0), (0, LANES - 2 * M_HEADS)))

    def pair_interleave(w):
        w5 = w.reshape(w.shape[0], -1, 2, 2, A_HEAD_DIM // 2)
        return jnp.swapaxes(w5, 2, 3).reshape(w.shape)

    w_cat = jnp.concatenate([w_in[:, :o_i], pair_interleave(w_in[:, o_qa:o_ka]),
                             pair_interleave(w_in[:, o_ka:o_ka + A_KV_WIDTH]),
                             w_in[:, o_ka + A_KV_WIDTH:], w_gate], axis=1)
    gate_bias = jnp.pad(jnp.concatenate([p["i_bias"], p["f_bias"]]),
                        (0, LANES - 2 * M_HEADS)).reshape(1, LANES)
    row = lambda v: v.reshape(1, -1)

    qk, vo, ha, gates = _in_proj(p["attn_sinks"], h, row(p["mix_pre_g"]), pos2, inv_freq,
                                 p["conv_qk"], w_cat, tm=512, seq=seq)
    norm_g_lanes = jnp.broadcast_to(p["mlstm_norm_g"][:, None], (M_WIDTH, LANES))
    h1, xq = _mlstm_mix(qk, vo, gates, gate_bias, norm_g_lanes, ha, h,
                        p["w_out"].astype(BF16), row(p["mix_post_g"]),
                        row(p["xattn_pre_g"]), p["w_xq"].astype(BF16), tm=512, seq=seq)
    kv = _mem_kv(mem2, row(p["mem_norm_g"]), p["w_xkv"].astype(BF16), tm=1024)
    return _xattn_ffn(h1, xq, kv, p["w_xo"], row(p["xattn_post_g"]),
                      row(p["ffn_pre_g"]), p["w_gate_up"], p["w_down"],
                      row(p["ffn_post_g"]), tm=512, seq=seq, mem_len=mem_len, ff_chunk=1536)


def kernel(x, mem, positions, mix_pre_g, mix_post_g, w_in, conv_qk, f_bias, i_bias, mlstm_norm_g,
           attn_sinks, w_out, xattn_pre_g, xattn_post_g, mem_norm_g, w_xq, w_xkv, w_xo,
           ffn_pre_g, ffn_post_g, w_gate_up, w_down):
    batch, seq, _ = x.shape
    mem_len = mem.shape[1]
    depth = w_in.shape[0]
    params = dict(mix_pre_g=mix_pre_g, mix_post_g=mix_post_g, w_in=w_in, conv_qk=conv_qk,
                  f_bias=f_bias, i_bias=i_bias, mlstm_norm_g=mlstm_norm_g, attn_sinks=attn_sinks,
                  w_out=w_out, xattn_pre_g=xattn_pre_g, xattn_post_g=xattn_post_g,
                  mem_norm_g=mem_norm_g, w_xq=w_xq, w_xkv=w_xkv, w_xo=w_xo, ffn_pre_g=ffn_pre_g,
                  ffn_post_g=ffn_post_g, w_gate_up=w_gate_up, w_down=w_down)
    inv = ROPE_THETA ** (-jnp.arange(0, A_HEAD_DIM, 2, dtype=F32) / A_HEAD_DIM)
    inv_freq = jnp.tile(inv, LANES // inv.shape[0]).reshape(1, LANES)
    h = x.reshape(batch * seq, D_MODEL)
    mem2 = mem.reshape(batch * mem_len, D_MODEL)
    pos2 = positions.reshape(batch * seq, 1)
    for l in range(depth):
        h = _layer(h, mem2, pos2, inv_freq, {k: v[l] for k, v in params.items()},
                   batch, seq, mem_len)
    return h.reshape(batch, seq, D_MODEL)
```

```python
import functools

import jax
import jax.numpy as jnp
from jax import lax
from jax.experimental import pallas as pl
from jax.experimental.pallas import tpu as pltpu

D_MODEL = 1024
EPS = 1e-6
M_HEADS = 4
M_HEAD_DIM = 256
M_WIDTH = M_HEADS * M_HEAD_DIM
M_CONV = 4
M_CHUNK = 128
A_Q_HEADS = 16
A_KV_HEADS = 2
A_HEAD_DIM = 64
A_WIDTH = A_Q_HEADS * A_HEAD_DIM
A_KV_WIDTH = A_KV_HEADS * A_HEAD_DIM
WINDOW = 128
ROPE_THETA = 10000.0
LOG2E = 1.4426950408889634
X_HEADS = 4
X_HEAD_DIM = D_MODEL // X_HEADS
LANES = 128
SUBLANES = 8
N_KEY_WINDOWS = 2 * A_KV_HEADS
CONV_HALO = SUBLANES
ROW_GROUPS = 2
WEIGHT_STAGE_CHUNKS = 8
SCORE_LOOKAHEAD = 3
EARLY_STREAM_CHUNKS = 2
VMEM_LIMIT = 56 * 1024 * 1024

F32 = jnp.float32
BF16 = jnp.bfloat16


def _rms(x, g):
    return x * lax.rsqrt(jnp.mean(x * x, axis=-1, keepdims=True) + EPS) * g


def _dot(a, b):
    return jnp.dot(a, b, preferred_element_type=F32)


def _dot_nt(a, b):
    return lax.dot_general(a, b, (((1,), (1,)), ((), ())), preferred_element_type=F32)


def _run_staggered(chains, start_per_round=1):
    pending = list(chains)
    running = []
    while pending or running:
        for _ in range(min(start_per_round, len(pending))):
            running.append(pending.pop(0))
        for gen in list(running):
            if next(gen, "done") == "done":
                running.remove(gen)


def _const_spec(shape):
    nd = len(shape)
    return pl.BlockSpec(shape, lambda *_: (0,) * nd, pipeline_mode=pl.Buffered(1))


def _in_proj_kernel(sink_ref, x_ref, g_ref, pos_ref, inv_ref, conv_ref, w_ref, qk_ref, vo_ref,
                    ha_ref, gate_ref, q_scr, k_ref, vt_ref, halo_ref, *, tiles_per_seq):
    W = WINDOW
    tm = x_ref.shape[0]
    n_blk = tm // W
    half = A_HEAD_DIM // 2
    first_tile = pl.program_id(0) % tiles_per_seq == 0

    @pl.when(first_tile)
    def _():
        for s in range(N_KEY_WINDOWS):
            k_ref[0, s * 2 * W:s * 2 * W + W, :] = jnp.zeros((W, LANES), BF16)
        vt_ref[0, :, 0:W] = jnp.zeros((LANES, W), BF16)
        halo_ref[...] = jnp.zeros_like(halo_ref)

    @pl.when(jnp.logical_not(first_tile))
    def _():
        for s in range(N_KEY_WINDOWS):
            k_ref[0, s * 2 * W:s * 2 * W + W, :] = k_ref[n_blk, s * 2 * W:s * 2 * W + W, :]
        vt_ref[0, :, 0:W] = vt_ref[n_blk, :, 0:W]

    u = _rms(x_ref[...], g_ref[...]).astype(BF16)

    stream_chunks = [(ref, c0, base + c0)
                     for c0 in range(0, 2 * M_WIDTH, 2 * LANES)
                     for ref, base in ((qk_ref, 0), (vo_ref, 2 * M_WIDTH))]

    def project_chunk():
        ref, c0, wc = stream_chunks.pop(0)
        t = _dot(u, w_ref[:, wc:wc + 2 * LANES])
        if ref is vo_ref:
            ref[:, c0:c0 + 2 * LANES] = t.astype(ref.dtype)
            return
        assert M_CONV == 4
        ext = jnp.concatenate([halo_ref[:, c0:c0 + 2 * LANES], t], axis=0)
        prev1 = pltpu.roll(ext, 1, 0)
        taps = [conv_ref[j:j + 1, c0:c0 + 2 * LANES] for j in range(M_CONV)]
        older = ext * taps[1] + prev1 * taps[0]
        acc = (t * taps[3] + prev1[CONV_HALO:, :] * taps[2]
               + pltpu.roll(older, 2, 0)[CONV_HALO:, :])
        halo_ref[:, c0:c0 + 2 * LANES] = t[tm - CONV_HALO:, :]
        scale = M_HEAD_DIM ** -0.5 if c0 >= M_WIDTH else 1.0
        half_x = acc * (0.5 * scale)
        arg = half_x if scale == 1.0 else half_x * (1.0 / scale)
        ref[:, c0:c0 + 2 * LANES] = (half_x * jnp.tanh(arg) + half_x).astype(ref.dtype)

    for _ in range(EARLY_STREAM_CHUNKS):
        project_chunk()
    off = 4 * M_WIDTH

    n_freq = A_HEAD_DIM // 2
    n_seg = LANES // n_freq
    rows = tm // n_seg
    seg = lax.broadcasted_iota(jnp.int32, (rows, LANES), 1) // n_freq
    pos_f = pos_ref[...].astype(F32)
    pos_c = jnp.broadcast_to(pos_f[0:rows], (rows, LANES))
    for s in range(1, n_seg):
        pos_c = jnp.where(seg == s, pos_f[s * rows:(s + 1) * rows], pos_c)
    ang_c = pos_c * inv_ref[...]

    def spread(table_c):
        rolled = [table_c] + [pltpu.roll(table_c, n_freq * k, 1) for k in range(1, n_seg)]
        chunks = []
        for s in range(n_seg):
            full = rolled[(0 - s) % n_seg]
            for p in range(1, n_seg):
                full = jnp.where(seg == p, rolled[(p - s) % n_seg], full)
            chunks.append(full)
        return jnp.concatenate(chunks, axis=0)

    lane = lax.broadcasted_iota(jnp.int32, (tm, LANES), 1)
    cos_t = spread(jnp.cos(ang_c))
    sin_t = spread(jnp.sin(ang_c))
    sin_t = jnp.where(lane < LANES // 2, -sin_t, sin_t)

    def rope(t, c, s):
        return t * c + pltpu.roll(t, LANES // 2, 1) * s

    scale = A_HEAD_DIM ** -0.5 * LOG2E
    cos_q = cos_t * scale
    sin_q = sin_t * scale
    o_kv = off + A_WIDTH
    kv = _dot(u, w_ref[:, o_kv:o_kv + 2 * LANES])
    k_rot = rope(kv[:, 0:LANES], cos_t, sin_t)
    slot_a = (lane % A_HEAD_DIM) < half
    variants = (jnp.where(slot_a, k_rot, 0.0),
                jnp.where(slot_a, 0.0, pltpu.roll(k_rot, half, 1)),
                jnp.where(slot_a, pltpu.roll(k_rot, LANES - half, 1), 0.0),
                jnp.where(slot_a, 0.0, k_rot))
    for j in range(n_blk):
        for s, var in enumerate(variants):
            blk = var[j * W:(j + 1) * W, :].astype(BF16)
            k_ref[j, s * 2 * W + W:(s + 1) * 2 * W, :] = blk
            k_ref[j + 1, s * 2 * W:s * 2 * W + W, :] = blk
        vt = kv[j * W:(j + 1) * W, LANES:2 * LANES].T.astype(BF16)
        vt_ref[j, :, W:2 * W] = vt
        vt_ref[j + 1, :, 0:W] = vt

    for j in range(A_WIDTH // (2 * LANES)):
        t = _dot(u, w_ref[:, off:off + 2 * LANES])
        for i in range(2):
            c0 = (2 * j + i) * LANES
            q_scr[:, c0:c0 + LANES] = rope(t[:, i * LANES:(i + 1) * LANES],
                                           cos_q, sin_q).astype(q_scr.dtype)
        off += 2 * LANES
    gate_ref[...] = _dot(u, w_ref[:, o_kv + 2 * LANES:o_kv + 3 * LANES])

    kpos = lax.broadcasted_iota(jnp.int32, (2 * W, W), 0)
    qpos = lax.broadcasted_iota(jnp.int32, (2 * W, W), 1)
    diff = W + qpos - kpos
    band = (diff >= 0) & (diff < W)
    bias_rest = jnp.where(band, 0.0, -jnp.inf).astype(F32)
    bias_first = jnp.where(band & (jnp.logical_not(first_tile) | (kpos >= W)),
                           0.0, -jnp.inf).astype(F32)

    n_pairs = A_Q_HEADS // 2
    pairs_per_group = n_pairs // A_KV_HEADS
    items = [(j, pair) for j in range(n_blk) for pair in range(n_pairs)]

    def scores(j, pair):
        g = pair // pairs_per_group
        q = q_scr[j * W:(j + 1) * W, pair * LANES:(pair + 1) * LANES]
        return _dot_nt(k_ref[j, g * 4 * W:(g + 1) * 4 * W, :], q)

    n_late = len(stream_chunks)
    pending = [scores(*it) for it in items[:SCORE_LOOKAHEAD]]
    for idx, (j, pair) in enumerate(items):
        g = pair // pairs_per_group
        if idx + SCORE_LOOKAHEAD < len(items):
            pending.append(scores(*items[idx + SCORE_LOOKAHEAD]))
        st_pair = pending.pop(0)
        bias = bias_first if j == 0 else bias_rest
        pes, inv_dens = [], []
        for slot in range(2):
            sink = sink_ref[2 * pair + slot] * LOG2E
            st = st_pair[slot * 2 * W:(slot + 1) * 2 * W, :] + bias
            mx = jnp.maximum(jnp.max(st, axis=0, keepdims=True), sink)
            pe = jnp.exp2(st - mx)
            den = jnp.sum(pe, axis=0, keepdims=True) + jnp.exp2(sink - mx)
            pes.append(pe.astype(BF16))
            inv_dens.append(1.0 / den)
        ot = _dot(vt_ref[j, g * A_HEAD_DIM:(g + 1) * A_HEAD_DIM, :],
                  jnp.concatenate(pes, axis=1))
        o_pair = jnp.concatenate([ot[:, slot * W:(slot + 1) * W] * inv_dens[slot]
                                  for slot in range(2)], axis=0)
        ha_ref[j * W:(j + 1) * W, pair * LANES:(pair + 1) * LANES] = (
            o_pair.T.astype(ha_ref.dtype))
        if (idx + 1) * n_late // len(items) > idx * n_late // len(items):
            project_chunk()
    assert not stream_chunks


def _in_proj(sinks, x2, g, pos, inv_freq, conv_qk, w_cat, tm, seq):
    t = x2.shape[0]
    widths = (2 * M_WIDTH, 2 * M_WIDTH, A_WIDTH, LANES)
    dtypes = (BF16, BF16, BF16, F32)
    n_blk = tm // WINDOW
    return pl.pallas_call(
        functools.partial(_in_proj_kernel, tiles_per_seq=seq // tm),
        grid=(t // tm,),
        in_specs=[pl.BlockSpec(memory_space=pltpu.SMEM),
                  pl.BlockSpec((tm, D_MODEL), lambda i: (i, 0)),
                  _const_spec((1, D_MODEL)),
                  pl.BlockSpec((tm, 1), lambda i: (i, 0)),
                  _const_spec((1, LANES)),
                  _const_spec(conv_qk.shape),
                  _const_spec(w_cat.shape)],
        out_specs=[pl.BlockSpec((tm, w), lambda i: (i, 0)) for w in widths],
        out_shape=[jax.ShapeDtypeStruct((t, w), d) for w, d in zip(widths, dtypes)],
        scratch_shapes=[pltpu.VMEM((tm, A_WIDTH), BF16),
                        pltpu.VMEM((n_blk + 1, N_KEY_WINDOWS * 2 * WINDOW, LANES), BF16),
                        pltpu.VMEM((n_blk + 1, LANES, 2 * WINDOW), BF16),
                        pltpu.VMEM((CONV_HALO, 2 * M_WIDTH), F32)],
        compiler_params=pltpu.CompilerParams(dimension_semantics=("arbitrary",),
                                             vmem_limit_bytes=VMEM_LIMIT),
        name="in_proj",
    )(sinks, x2, g, pos, inv_freq, conv_qk, w_cat)


def _mlstm_mix_kernel(qk_ref, vo_ref, gate_ref, bias_ref, ngb_ref, ha_ref, x_ref,
                      wo_ref, gpost_ref, gpre_ref, wq_ref, h1_ref, xq_ref,
                      ct_ref, n_ref, m_ref, hm_ref, *, tiles_per_seq):
    L = M_CHUNK
    D = M_HEAD_DIM
    tm = qk_ref.shape[0]
    step = pl.program_id(0)
    write_slot = step % 2
    read_slot = (step + 1) % 2

    @pl.when(step % tiles_per_seq == 0)
    def _():
        ct_ref[...] = jnp.zeros_like(ct_ref)
        n_ref[...] = jnp.zeros_like(n_ref)
        m_ref[...] = jnp.zeros_like(m_ref)

    @pl.when(step == 0)
    def _():
        hm_ref[1] = jnp.zeros(hm_ref.shape[1:], hm_ref.dtype)

    row = lax.broadcasted_iota(jnp.int32, (L, L), 0)
    col = lax.broadcasted_iota(jnp.int32, (L, L), 1)
    tril = jnp.where(col <= row, 1.0, 0.0).astype(F32)
    key_le_query = row <= col
    sub8 = lax.broadcasted_iota(jnp.int32, (SUBLANES, 1), 0)

    def gate_terms(c):
        gates = gate_ref[c * L:(c + 1) * L, :] + bias_ref[...]
        bcum = jnp.dot(tril, jax.nn.log_sigmoid(gates), precision=lax.Precision.HIGHEST,
                       preferred_element_type=F32)
        return gates, bcum, gates.T, bcum.T

    def hi_lo_rows(v):
        hi = v.astype(BF16).astype(F32)
        return jnp.where(sub8 == 0, hi, jnp.where(sub8 == 1, v - hi, 0.0)).astype(BF16)

    def load_qk(c, h):
        qb = qk_ref[c * L:(c + 1) * L, h * D:(h + 1) * D]
        kb = qk_ref[c * L:(c + 1) * L, M_WIDTH + h * D:M_WIDTH + (h + 1) * D]
        return qb, kb, _dot_nt(kb, qb)

    def head_chain(h, all_terms):
        n_chunks = len(all_terms)
        nxt = load_qk(0, h)
        yield
        for c, terms in enumerate(all_terms):
            qb, kb, st = nxt
            inter_t = _dot_nt(ct_ref[h].astype(BF16), qb)
            qn2 = _dot_nt(hi_lo_rows(n_ref[h:h + 1, :]), qb)
            yield
            finish = chunk_body(c, h, terms, qb, kb, st, inter_t, qn2)
            next(finish)
            if c + 1 < n_chunks:
                nxt = load_qk(c + 1, h)
            yield
            next(finish, None)
            yield

    def chunk_body(c, h, terms, qb, kb, st, inter_t, qn2):
        gates, bcum, gates_t, bcum_t = terms
        c0 = h * D
        sr = h
        b_row = bcum_t[M_HEADS + h:M_HEADS + h + 1, :]
        i_row = gates_t[h:h + 1, :]
        c_col = gates[:, h:h + 1] - bcum[:, M_HEADS + h:M_HEADS + h + 1]
        m_prev = m_ref[sr:sr + 1, 0:1]
        g_row = b_row + m_prev
        dm_t = jnp.where(key_le_query, c_col + b_row, -jnp.inf)
        mj = jnp.maximum(g_row, jnp.max(dm_t, axis=0, keepdims=True))
        s_t = st * jnp.exp(dm_t - mj)
        vt = vo_ref[c * L:(c + 1) * L, c0:c0 + D].astype(F32).T
        intra_t = _dot(vt.astype(BF16), s_t.astype(BF16))
        b_last = b_row[:, L - 1:L]
        w_row = b_last - b_row + i_row
        m_new = jnp.maximum(b_last + m_prev, jnp.max(w_row, axis=1, keepdims=True))
        decay = jnp.exp(b_last + m_prev - m_new)
        wi_row = jnp.exp(w_row - m_new)
        ct_new = _dot((vt * wi_row).astype(BF16), kb)
        n_new2 = _dot(hi_lo_rows(wi_row), kb)
        yield
        wg = jnp.exp(g_row - mj)
        num_t = wg * inter_t + intra_t
        den = wg * (qn2[0:1, :] + qn2[1:2, :]) + jnp.sum(s_t, axis=0, keepdims=True)
        hh_t = num_t * (1.0 / jnp.maximum(jnp.abs(den), jnp.exp(-mj)))
        rs = lax.rsqrt(jnp.mean(hh_t * hh_t, axis=0, keepdims=True) + EPS)
        hn_t = hh_t * rs * ngb_ref[c0:c0 + D, :]
        og = vo_ref[c * L:(c + 1) * L, M_WIDTH + c0:M_WIDTH + c0 + D]
        sig = jnp.tanh(og * 0.5) * 0.5 + 0.5
        hm_ref[write_slot, c * L:(c + 1) * L, c0:c0 + D] = sig * hn_t.T.astype(BF16)
        ct_ref[sr] = decay * ct_ref[sr] + ct_new
        n_ref[sr:sr + 1, :] = decay * n_ref[sr:sr + 1, :] + n_new2[0:1, :] + n_new2[1:2, :]
        m_ref[sr:sr + 1, :] = jnp.broadcast_to(m_new, (1, LANES))

    def mix_chain():
        n_col = 2 * LANES
        parts = []
        for c0 in range(0, D_MODEL, n_col):
            part = _dot(hm_ref[read_slot], wo_ref[0:M_WIDTH, c0:c0 + n_col])
            yield
            parts.append(part + _dot(ha_ref[...], wo_ref[M_WIDTH:, c0:c0 + n_col]))
            yield
        h1 = x_ref[...] + _rms(jnp.concatenate(parts, axis=1), gpost_ref[...])
        h1_ref[...] = h1
        u = _rms(h1, gpre_ref[...]).astype(BF16)
        yield
        for c0 in range(0, D_MODEL, n_col):
            xq_ref[:, c0:c0 + n_col] = (_dot(u, wq_ref[:, c0:c0 + n_col])
                                        * (X_HEAD_DIM ** -0.5 * LOG2E)).astype(xq_ref.dtype)
            yield

    all_terms = [gate_terms(c) for c in range(tm // L)]
    chains = [head_chain(h, all_terms) for h in range(M_HEADS)] + [mix_chain()]
    _run_staggered(chains, start_per_round=len(chains))


def _mlstm_mix(qk, vo, gates, gate_bias, norm_g_lanes, ha, x2, w_out, g_post, g_pre, w_xq,
               tm, seq):
    t = x2.shape[0]
    n_tiles = t // tm
    cur = lambda s: (jnp.minimum(s, n_tiles - 1), 0)
    prev = lambda s: (jnp.maximum(s - 1, 0), 0)
    return pl.pallas_call(
        functools.partial(_mlstm_mix_kernel, tiles_per_seq=seq // tm),
        grid=(n_tiles + 1,),
        in_specs=[pl.BlockSpec((tm, 2 * M_WIDTH), cur),
                  pl.BlockSpec((tm, 2 * M_WIDTH), cur),
                  pl.BlockSpec((tm, LANES), cur),
                  _const_spec((1, LANES)),
                  _const_spec((M_WIDTH, LANES)),
                  pl.BlockSpec((tm, A_WIDTH), prev),
                  pl.BlockSpec((tm, D_MODEL), prev),
                  _const_spec(w_out.shape),
                  _const_spec((1, D_MODEL)), _const_spec((1, D_MODEL)),
                  _const_spec(w_xq.shape)],
        out_specs=[pl.BlockSpec((tm, D_MODEL), prev), pl.BlockSpec((tm, D_MODEL), prev)],
        out_shape=[jax.ShapeDtypeStruct((t, D_MODEL), F32),
                   jax.ShapeDtypeStruct((t, D_MODEL), BF16)],
        scratch_shapes=[pltpu.VMEM((M_HEADS, M_HEAD_DIM, M_HEAD_DIM), F32),
                        pltpu.VMEM((M_HEADS, M_HEAD_DIM), F32),
                        pltpu.VMEM((M_HEADS, LANES), F32),
                        pltpu.VMEM((2, tm, M_WIDTH), BF16)],
        compiler_params=pltpu.CompilerParams(dimension_semantics=("arbitrary",),
                                             vmem_limit_bytes=VMEM_LIMIT),
        name="mlstm_mix",
    )(qk, vo, gates, gate_bias, norm_g_lanes, ha, x2, w_out, g_post, g_pre, w_xq)


def _mem_kv_kernel(mem_ref, g_ref, w_ref, kv_ref):
    u = _rms(mem_ref[...], g_ref[...]).astype(BF16)
    kv_ref[...] = _dot(u, w_ref[...]).astype(kv_ref.dtype)


def _mem_kv(mem2, g, w_xkv, tm):
    t = mem2.shape[0]
    assert t % tm == 0
    return pl.pallas_call(
        _mem_kv_kernel,
        grid=(t // tm,),
        in_specs=[pl.BlockSpec((tm, D_MODEL), lambda i: (i, 0)),
                  _const_spec((1, D_MODEL)),
                  _const_spec(w_xkv.shape)],
        out_specs=pl.BlockSpec((tm, 2 * D_MODEL), lambda i: (i, 0)),
        out_shape=jax.ShapeDtypeStruct((t, 2 * D_MODEL), BF16),
        compiler_params=pltpu.CompilerParams(dimension_semantics=("arbitrary",),
                                             vmem_limit_bytes=VMEM_LIMIT),
        name="mem_kv",
    )(mem2, g, w_xkv)


def _load_cast(src_hbm, dst_ref, stage_ref, sem_ref):
    rows = stage_ref.shape[1]
    n = src_hbm.shape[0] // rows

    def copy(c):
        return pltpu.make_async_copy(src_hbm.at[pl.ds(c * rows, rows)], stage_ref.at[c % 2],
                                     sem_ref.at[c % 2])

    copy(0).start()
    yield
    for c in range(n):
        if c + 1 < n:
            copy(c + 1).start()
        copy(c).wait()
        dst_ref[c * rows:(c + 1) * rows, :] = stage_ref[c % 2].astype(dst_ref.dtype)
        yield


def _xattn_ffn_kernel(h1_ref, xq_ref, kv_ref, wo_hbm, gxpost_ref, gfpre_ref, wgu_hbm, wd_hbm,
                      gfpost_ref, out_ref, o_ref, wo_ref, wgu_ref, wd_ref, stage_o, stage_gu,
                      stage_d, sem_o, sem_gu, sem_d, *, ff_chunk):
    @pl.when(pl.program_id(0) == 0)
    def _():
        _run_staggered([_load_cast(wgu_hbm, wgu_ref, stage_gu, sem_gu),
                        _load_cast(wd_hbm, wd_ref, stage_d, sem_d),
                        _load_cast(wo_hbm, wo_ref, stage_o, sem_o)], start_per_round=3)

    d_ff = wd_ref.shape[0]
    chunks = [(j0, min(j0 + ff_chunk, d_ff)) for j0 in range(0, d_ff, ff_chunk)]

    def rows_stages(r0, r1):
        heads = [h * X_HEAD_DIM for h in range(X_HEADS)]
        scores = [_dot_nt(xq_ref[r0:r1, c0:c0 + X_HEAD_DIM], kv_ref[:, c0:c0 + X_HEAD_DIM])
                  for c0 in heads]
        yield
        for c0, s in zip(heads, scores):
            e = jnp.exp2(s - jnp.max(s, axis=1, keepdims=True))
            inv = 1.0 / jnp.sum(e, axis=1, keepdims=True)
            o = _dot(e.astype(BF16), kv_ref[:, D_MODEL + c0:D_MODEL + c0 + X_HEAD_DIM])
            o_ref[r0:r1, c0:c0 + X_HEAD_DIM] = (o * inv).astype(BF16)
        yield
        c = _dot(o_ref[r0:r1, :], wo_ref[...])
        yield
        h2 = h1_ref[r0:r1, :] + _rms(c, gxpost_ref[...])
        u = _rms(h2, gfpre_ref[...]).astype(BF16)
        f = None
        for j0, j1 in chunks:
            gate = _dot(u, wgu_ref[:, j0:j1])
            up = _dot(u, wgu_ref[:, d_ff + j0:d_ff + j1])
            yield
            act = (gate * jax.nn.sigmoid(gate) * up).astype(BF16)
            part = _dot(act, wd_ref[j0:j1, :])
            f = part if f is None else f + part
        yield
        out_ref[r0:r1, :] = h2 + _rms(f, gfpost_ref[...])

    tm = h1_ref.shape[0]
    step = tm // ROW_GROUPS
    _run_staggered([rows_stages(r0, r0 + step) for r0 in range(0, tm, step)],
                   start_per_round=ROW_GROUPS)


def _xattn_ffn(h1, xq, kv, w_xo, g_xpost, g_fpre, w_gu, w_d, g_fpost, tm, seq, mem_len, ff_chunk):
    t = h1.shape[0]
    row = lambda i: (i, 0)
    per_b = seq // tm
    return pl.pallas_call(
        functools.partial(_xattn_ffn_kernel, ff_chunk=ff_chunk),
        grid=(t // tm,),
        in_specs=[pl.BlockSpec((tm, D_MODEL), row),
                  pl.BlockSpec((tm, D_MODEL), row),
                  pl.BlockSpec((mem_len, 2 * D_MODEL), lambda i: (i // per_b, 0)),
                  pl.BlockSpec(memory_space=pl.ANY),
                  _const_spec((1, D_MODEL)), _const_spec((1, D_MODEL)),
                  pl.BlockSpec(memory_space=pl.ANY), pl.BlockSpec(memory_space=pl.ANY),
                  _const_spec((1, D_MODEL))],
        out_specs=pl.BlockSpec((tm, D_MODEL), row),
        out_shape=jax.ShapeDtypeStruct((t, D_MODEL), F32),
        scratch_shapes=[pltpu.VMEM((tm, D_MODEL), BF16),
                        pltpu.VMEM(w_xo.shape, BF16), pltpu.VMEM(w_gu.shape, BF16),
                        pltpu.VMEM(w_d.shape, BF16),
                        pltpu.VMEM((2, w_xo.shape[0] // WEIGHT_STAGE_CHUNKS, w_xo.shape[1]), F32),
                        pltpu.VMEM((2, w_gu.shape[0] // (4 * WEIGHT_STAGE_CHUNKS), w_gu.shape[1]),
                                   F32),
                        pltpu.VMEM((2, w_d.shape[0] // (2 * WEIGHT_STAGE_CHUNKS), w_d.shape[1]),
                                   F32),
                        pltpu.SemaphoreType.DMA((2,)), pltpu.SemaphoreType.DMA((2,)),
                        pltpu.SemaphoreType.DMA((2,))],
        compiler_params=pltpu.CompilerParams(dimension_semantics=("arbitrary",),
                                             vmem_limit_bytes=VMEM_LIMIT),
        name="xattn_ffn",
    )(h1, xq, kv, w_xo, g_xpost, g_fpre, w_gu, w_d, g_fpost)


def _layer(h, mem2, pos2, inv_freq, p, batch, seq, mem_len):
    w_in = p["w_in"].astype(BF16)
    o_i = 4 * M_WIDTH
    o_qa = o_i + 2 * M_HEADS
    o_ka = o_qa + A_WIDTH
    w_gate = jnp.pad(w_in[:, o_i:o_qa], ((0, 0), (0, LANES - 2 * M_HEADS)))

    def pair_interleave(w):
        w5 = w.reshape(w.shape[0], -1, 2, 2, A_HEAD_DIM // 2)
        return jnp.swapaxes(w5, 2, 3).reshape(w.shape)

    w_cat = jnp.concatenate([w_in[:, :o_i], pair_interleave(w_in[:, o_qa:o_ka]),
                             pair_interleave(w_in[:, o_ka:o_ka + A_KV_WIDTH]),
                             w_in[:, o_ka + A_KV_WIDTH:], w_gate], axis=1)
    gate_bias = jnp.pad(jnp.concatenate([p["i_bias"], p["f_bias"]]),
                        (0, LANES - 2 * M_HEADS)).reshape(1, LANES)
    row = lambda v: v.reshape(1, -1)

    qk, vo, ha, gates = _in_proj(p["attn_sinks"], h, row(p["mix_pre_g"]), pos2, inv_freq,
                                 p["conv_qk"], w_cat, tm=512, seq=seq)
    norm_g_lanes = jnp.broadcast_to(p["mlstm_norm_g"][:, None], (M_WIDTH, LANES))
    h1, xq = _mlstm_mix(qk, vo, gates, gate_bias, norm_g_lanes, ha, h,
                        p["w_out"].astype(BF16), row(p["mix_post_g"]),
                        row(p["xattn_pre_g"]), p["w_xq"].astype(BF16), tm=512, seq=seq)
    kv = _mem_kv(mem2, row(p["mem_norm_g"]), p["w_xkv"].astype(BF16), tm=1024)
    return _xattn_ffn(h1, xq, kv, p["w_xo"], row(p["xattn_post_g"]),
                      row(p["ffn_pre_g"]), p["w_gate_up"], p["w_down"],
                      row(p["ffn_post_g"]), tm=512, seq=seq, mem_len=mem_len, ff_chunk=1536)


def kernel(x, mem, positions, mix_pre_g, mix_post_g, w_in, conv_qk, f_bias, i_bias, mlstm_norm_g,
           attn_sinks, w_out, xattn_pre_g, xattn_post_g, mem_norm_g, w_xq, w_xkv, w_xo,
           ffn_pre_g, ffn_post_g, w_gate_up, w_down):
    batch, seq, _ = x.shape
    mem_len = mem.shape[1]
    depth = w_in.shape[0]
    params = dict(mix_pre_g=mix_pre_g, mix_post_g=mix_post_g, w_in=w_in, conv_qk=conv_qk,
                  f_bias=f_bias, i_bias=i_bias, mlstm_norm_g=mlstm_norm_g, attn_sinks=attn_sinks,
                  w_out=w_out, xattn_pre_g=xattn_pre_g, xattn_post_g=xattn_post_g,
                  mem_norm_g=mem_norm_g, w_xq=w_xq, w_xkv=w_xkv, w_xo=w_xo, ffn_pre_g=ffn_pre_g,
                  ffn_post_g=ffn_post_g, w_gate_up=w_gate_up, w_down=w_down)
    inv = ROPE_THETA ** (-jnp.arange(0, A_HEAD_DIM, 2, dtype=F32) / A_HEAD_DIM)
    inv_freq = jnp.tile(inv, LANES // inv.shape[0]).reshape(1, LANES)
    h = x.reshape(batch * seq, D_MODEL)
    mem2 = mem.reshape(batch * mem_len, D_MODEL)
    pos2 = positions.reshape(batch * seq, 1)
    for l in range(depth):
        h = _layer(h, mem2, pos2, inv_freq, {k: v[l] for k, v in params.items()},
                   batch, seq, mem_len)
    return h.reshape(batch, seq, D_MODEL)
```
